```python
import math
import jax, jax.numpy as jnp
from jax import lax
import numpy as np

D_MODEL = 2048
BATCH = 4
SEQ = 2048
DEPTH = 1
DEC_BATCH = 128
DEC_SEQ = 8
PAST_LEN = 16384
PAGE_SIZE = 128

MIX_WIDTH = D_MODEL
RET_HEADS = 4
RET_DK = 256
RET_DV = 256
RET_WIDTH = RET_HEADS * RET_DV
RET_CHUNK = 128
ROPE_BASE = 10000.0
POOL_WINDOWS = (2, 4, 8, 16)
POOL_GROUPS = len(POOL_WINDOWS)
POOL_WIDTH = MIX_WIDTH - RET_WIDTH
POOL_GC = POOL_WIDTH // POOL_GROUPS
POOL_BUF = max(POOL_WINDOWS) - 1
D_FF = int(math.ceil(8 * D_MODEL / 3 / 256) * 256)
IN_WIDTH = 2 * RET_HEADS * RET_DK + 2 * RET_WIDTH + POOL_WIDTH
EPS = 1e-6

kernel_name = "hybrid_retention_pool_decoder_step"


def rms_norm(x, w):
    xf = x.astype(jnp.float32)
    y = xf * lax.rsqrt(jnp.mean(xf * xf, axis=-1, keepdims=True) + EPS)
    return (y * w.astype(jnp.float32)).astype(x.dtype)


def rope(x, pos):
    d = x.shape[-1]
    inv_freq = 1.0 / (ROPE_BASE ** (jnp.arange(0, d, 2, dtype=jnp.float32) / d))
    ang = pos[:, None] * inv_freq[None, :]
    cos = jnp.cos(ang)[None, :, None, :]
    sin = jnp.sin(ang)[None, :, None, :]
    x1, x2 = x[..., : d // 2], x[..., d // 2:]
    return jnp.concatenate([x1 * cos - x2 * sin, x2 * cos + x1 * sin], axis=-1)


def retention_chunk(S, qkv, log_gamma):
    q, k, v = qkv
    C = q.shape[2]
    idx = jnp.arange(C, dtype=jnp.float32)
    diff = idx[:, None] - idx[None, :]
    lg = log_gamma[:, None, None]
    decay = jnp.where(diff[None] >= 0, jnp.exp(jnp.maximum(diff[None], 0.0) * lg), 0.0)
    scores = jnp.einsum('bhid,bhjd->bhij', q, k) * decay[None]
    o_inner = jnp.einsum('bhij,bhje->bhie', scores, v)
    q_dec = jnp.exp((idx[None, :] + 1.0) * log_gamma[:, None])
    o_cross = jnp.einsum('bhid,bhde->bhie', q, S) * q_dec[None, :, :, None]
    k_dec = jnp.exp((C - 1.0 - idx[None, :]) * log_gamma[:, None])
    S_new = jnp.exp(C * log_gamma)[None, :, None, None] * S + jnp.einsum(
        'bhjd,bhje->bhde', k * k_dec[None, :, :, None], v)
    return S_new, o_inner + o_cross


def retention(q, k, v, S0):
    B, H, L, dk = q.shape
    dv = v.shape[-1]
    log_gamma = jnp.log(1.0 - 2.0 ** (-5.0 - jnp.arange(H, dtype=jnp.float32)))
    C = RET_CHUNK if L % RET_CHUNK == 0 else L
    n = L // C

    def to_chunks(t):
        return t.reshape(B, H, n, C, t.shape[-1]).transpose(2, 0, 1, 3, 4)

    S_fin, o = lax.scan(lambda S, c: retention_chunk(S, c, log_gamma), S0,
                        (to_chunks(q), to_chunks(k), to_chunks(v)))
    o = o.transpose(1, 2, 0, 3, 4).reshape(B, H, L, dv)
    return o, S_fin


def multiscale_pool(u, buf, start):
    B, L, _ = u.shape
    xp = jnp.concatenate([buf.astype(jnp.float32), u.astype(jnp.float32)], axis=1)
    cs = jnp.concatenate([jnp.zeros((B, 1, POOL_WIDTH), jnp.float32),
                          jnp.cumsum(xp, axis=1)], axis=1)
    pos = start + jnp.arange(L, dtype=jnp.float32)
    outs = []
    for g, w in enumerate(POOL_WINDOWS):
        sl = slice(g * POOL_GC, (g + 1) * POOL_GC)
        hi = cs[:, POOL_BUF + 1:POOL_BUF + 1 + L, sl]
        lo = cs[:, POOL_BUF + 1 - w:POOL_BUF + 1 - w + L, sl]
        cnt = jnp.minimum(pos + 1.0, float(w))[None, :, None]
        outs.append((hi - lo) / cnt)
    mean = jnp.concatenate(outs, axis=-1)
    pooled = mean - u.astype(jnp.float32)
    new_buf = xp[:, -POOL_BUF:, :]
    return pooled, new_buf


def hybrid_layer(x, S0, pool_buf, start, norm_mix_pre, norm_mix_post, w_in, ret_norm_w,
                 w_pool, pool_scale, w_out, norm_ffn_pre, norm_ffn_post, w_gate, w_up, w_down):
    B, L, _ = x.shape
    pos = start + jnp.arange(L, dtype=jnp.float32)
    h = rms_norm(x, norm_mix_pre)
    proj = h @ w_in
    o1 = RET_HEADS * RET_DK
    o2 = 2 * o1
    o3 = o2 + RET_WIDTH
    o4 = o3 + RET_WIDTH
    q = proj[..., :o1].astype(jnp.float32).reshape(B, L, RET_HEADS, RET_DK)
    k = proj[..., o1:o2].astype(jnp.float32).reshape(B, L, RET_HEADS, RET_DK)
    v = proj[..., o2:o3].astype(jnp.float32).reshape(B, L, RET_HEADS, RET_DV)
    g = proj[..., o3:o4]
    u = proj[..., o4:]
    q = rope(q, pos)
    k = rope(k, pos) * (RET_DK ** -0.5)
    o, S_new = retention(q.transpose(0, 2, 1, 3), k.transpose(0, 2, 1, 3),
                         v.transpose(0, 2, 1, 3), S0.astype(jnp.float32))
    o = o.transpose(0, 2, 1, 3)
    o = o * lax.rsqrt(jnp.mean(o * o, axis=-1, keepdims=True) + EPS)
    o = o * ret_norm_w.astype(jnp.float32).reshape(RET_HEADS, RET_DV)
    ret_out = (jax.nn.silu(g.astype(jnp.float32)) * o.reshape(B, L, RET_WIDTH)).astype(x.dtype)

    pooled, new_buf = multiscale_pool(u, pool_buf, start)
    pooled = pooled.astype(x.dtype).reshape(B, L, POOL_GROUPS, POOL_GC)
    pool_out = jnp.einsum('blgc,gcd->blgd', pooled, w_pool).reshape(B, L, POOL_WIDTH) * pool_scale

    mix = jnp.concatenate([ret_out, pool_out.astype(x.dtype)], axis=-1) @ w_out
    x = x + rms_norm(mix, norm_mix_post)
    hf = rms_norm(x, norm_ffn_pre)
    ff = (jax.nn.silu(hf @ w_gate) * (hf @ w_up)) @ w_down
    x = x + rms_norm(ff, norm_ffn_post)
    return x, S_new, new_buf


def setup_inputs(seed: int = 0) -> dict:
    key = jax.random.key(seed)
    ks = jax.random.split(key, 20)
    f32 = jnp.float32

    def nrm(k, shape, scale):
        return jax.random.normal(k, shape, f32) * scale

    return {
        "x_prompt": nrm(ks[0], (BATCH, SEQ, D_MODEL), 1.0),
        "x_sample": nrm(ks[1], (DEC_BATCH, DEC_SEQ, D_MODEL), 1.0),
        "state_ret": nrm(ks[2], (DEC_BATCH, RET_HEADS, RET_DK, RET_DV), 0.1),
        "state_pool": nrm(ks[3], (DEC_BATCH, POOL_BUF, POOL_WIDTH), 1.0),
        "norm_mix_pre": 1.0 + nrm(ks[4], (D_MODEL,), 0.02),
        "norm_mix_post": 1.0 + nrm(ks[5], (D_MODEL,), 0.02),
        "w_in": nrm(ks[6], (D_MODEL, IN_WIDTH), D_MODEL ** -0.5),
        "ret_norm_w": 1.0 + nrm(ks[7], (RET_WIDTH,), 0.02),
        "w_pool": nrm(ks[8], (POOL_GROUPS, POOL_GC, POOL_GC), POOL_GC ** -0.5),
        "pool_scale": 1.0 + nrm(ks[9], (POOL_WIDTH,), 0.02),
        "w_out": nrm(ks[10], (MIX_WIDTH, D_MODEL), MIX_WIDTH ** -0.5),
        "norm_ffn_pre": 1.0 + nrm(ks[11], (D_MODEL,), 0.02),
        "norm_ffn_post": 1.0 + nrm(ks[12], (D_MODEL,), 0.02),
        "w_gate": nrm(ks[13], (D_MODEL, D_FF), D_MODEL ** -0.5),
        "w_up": nrm(ks[14], (D_MODEL, D_FF), D_MODEL ** -0.5),
        "w_down": nrm(ks[15], (D_FF, D_MODEL), D_FF ** -0.5),
    }


def reference(x_prompt, x_sample, state_ret, state_pool, norm_mix_pre, norm_mix_post, w_in,
              ret_norm_w, w_pool, pool_scale, w_out, norm_ffn_pre, norm_ffn_post,
              w_gate, w_up, w_down):
    weights = (norm_mix_pre, norm_mix_post, w_in, ret_norm_w, w_pool, pool_scale, w_out,
               norm_ffn_pre, norm_ffn_post, w_gate, w_up, w_down)
    yp = x_prompt
    Sp = jnp.zeros((BATCH, RET_HEADS, RET_DK, RET_DV), jnp.float32)
    bp = jnp.zeros((BATCH, POOL_BUF, POOL_WIDTH), x_prompt.dtype)
    for _ in range(DEPTH):
        yp, Sp, bp = hybrid_layer(yp, Sp, bp, 0.0, *weights)
    ys = x_sample
    Ss, bs = state_ret, state_pool
    for _ in range(DEPTH):
        ys, Ss, bs = hybrid_layer(ys, Ss, bs, float(PAST_LEN), *weights)
    new_ret_prompt = Sp.astype(x_prompt.dtype)
    new_pool_prompt = bp.astype(x_prompt.dtype)
    new_ret_sample = Ss.astype(state_ret.dtype)
    new_pool_sample = bs.astype(state_pool.dtype)
    return (yp, ys, new_ret_prompt, new_pool_prompt, new_ret_sample, new_pool_sample)
```

```python
import functools
import math

import jax
import jax.numpy as jnp
from jax import lax
from jax.experimental import pallas as pl
from jax.experimental.pallas import tpu as pltpu

F32 = jnp.float32
BF16 = jnp.bfloat16

RET_HEADS = 4
RET_DK = 256
RET_DV = 256
RET_WIDTH = RET_HEADS * RET_DV
RET_CHUNK = 128
ROPE_BASE = 10000.0
POOL_WINDOWS = (2, 4, 8, 16)
POOL_GC = 256
POOL_WIDTH = POOL_GC * len(POOL_WINDOWS)
POOL_BUF = max(POOL_WINDOWS) - 1
POOL_HIST = POOL_BUF + 1
PAST_LEN = 16384
EPS = 1e-6

VMEM_LIMIT_BYTES = 56 * 1024 * 1024


def _compiler_params(semantics):
    return pltpu.CompilerParams(dimension_semantics=semantics,
                                vmem_limit_bytes=VMEM_LIMIT_BYTES)


def _rms_rows(x, w):
    ms = jnp.mean(x * x, axis=-1, keepdims=True)
    return (x * lax.rsqrt(ms + EPS)) * w


def _silu(g):
    return g * (1.0 / (1.0 + jnp.exp(-g)))


def _norm_to_scratch(x_ref, w_ref, h_ref, rows, chunk):
    def body(r, carry):
        sl = pl.ds(pl.multiple_of(r * chunk, chunk), chunk)
        h_ref[sl, :] = _rms_rows(x_ref[sl, :], w_ref[...]).astype(BF16)
        return carry
    lax.fori_loop(0, rows // chunk, body, 0)


def _inproj_kernel(x_ref, nw_ref, w_ref, o_ref, h_ref, *, tm, chunk):
    @pl.when(pl.program_id(1) == 0)
    def _():
        _norm_to_scratch(x_ref, nw_ref, h_ref, tm, chunk)
    o_ref[...] = jnp.dot(h_ref[...], w_ref[...].astype(BF16),
                         preferred_element_type=F32)


def _inproj(x2, norm_w, w_in, *, tm, tn):
    m, d = x2.shape
    n = w_in.shape[1]
    return pl.pallas_call(
        functools.partial(_inproj_kernel, tm=tm, chunk=128),
        grid=(m // tm, n // tn),
        in_specs=[
            pl.BlockSpec((tm, d), lambda i, j: (i, 0)),
            pl.BlockSpec((1, d), lambda i, j: (0, 0)),
            pl.BlockSpec((d, tn), lambda i, j: (0, j)),
        ],
        out_specs=pl.BlockSpec((tm, tn), lambda i, j: (i, j)),
        out_shape=jax.ShapeDtypeStruct((m, n), F32),
        scratch_shapes=[pltpu.VMEM((tm, d), BF16)],
        compiler_params=_compiler_params(("parallel", "arbitrary")),
        name="inproj",
    )(x2, norm_w.reshape(1, d), w_in)


def _rope(x, cos, sin):
    half = x.shape[-1] // 2
    x1, x2 = x[:, :half], x[:, half:]
    return jnp.concatenate([x1 * cos - x2 * sin, x2 * cos + x1 * sin], axis=-1)


def _retention_kernel(q_ref, k_ref, v_ref, g_ref, s0_ref, cos_ref, sin_ref, nw_ref,
                      o_ref, s_ref, *, chunk, log_gammas):
    @pl.when(pl.program_id(1) == 0)
    def _():
        s_ref[...] = s0_ref[...]

    cos = cos_ref[...]
    sin = sin_ref[...]
    row = lax.broadcasted_iota(jnp.int32, (chunk, chunk), 0)
    col = lax.broadcasted_iota(jnp.int32, (chunk, chunk), 1)
    diff = (row - col).astype(F32)
    pos = lax.broadcasted_iota(jnp.int32, (chunk, 1), 0).astype(F32)

    for h, lg in enumerate(log_gammas):
        cs = slice(h * RET_DK, (h + 1) * RET_DK)
        qr = _rope(q_ref[:, cs], cos, sin)
        kr = _rope(k_ref[:, cs], cos, sin) * (RET_DK ** -0.5)
        vb = v_ref[:, cs].astype(BF16)
        s_old = s_ref[h]

        decay = jnp.where(diff >= 0, jnp.exp(jnp.maximum(diff, 0.0) * lg), 0.0)
        qb = qr.astype(BF16)
        scores = lax.dot_general(qb, kr.astype(BF16), (((1,), (1,)), ((), ())),
                                 preferred_element_type=F32) * decay
        o_inner = jnp.dot(scores.astype(BF16), vb, preferred_element_type=F32)
        q_dec = jnp.exp((pos + 1.0) * lg)
        o_cross = jnp.dot(qb, s_old.astype(BF16), preferred_element_type=F32) * q_dec
        k_dec = jnp.exp((chunk - 1.0 - pos) * lg)
        kd = (kr * k_dec).astype(BF16)
        s_ref[h] = math.exp(chunk * lg) * s_old + lax.dot_general(
            kd, vb, (((0,), (0,)), ((), ())), preferred_element_type=F32)

        o = o_inner + o_cross
        o = o * lax.rsqrt(jnp.mean(o * o, axis=-1, keepdims=True) + EPS)
        o = o * nw_ref[:, cs]
        o_ref[:, cs] = (_silu(g_ref[:, cs]) * o).astype(BF16)


def _retention(proj3, s0, cos, sin, ret_norm_w):
    b, l, _ = proj3.shape
    chunk = RET_CHUNK if l % RET_CHUNK == 0 else l
    log_gammas = tuple(math.log(1.0 - 2.0 ** (-5.0 - h)) for h in range(RET_HEADS))
    half = RET_DK // 2

    def col_block(idx):
        return pl.BlockSpec((None, chunk, RET_WIDTH), lambda i, c: (i, c, idx))

    state_spec = pl.BlockSpec((None, RET_HEADS, RET_DK, RET_DV), lambda i, c: (i, 0, 0, 0))
    return pl.pallas_call(
        functools.partial(_retention_kernel, chunk=chunk, log_gammas=log_gammas),
        grid=(b, l // chunk),
        in_specs=[
            col_block(0), col_block(1), col_block(2), col_block(3),
            state_spec,
            pl.BlockSpec((chunk, half), lambda i, c: (c, 0)),
            pl.BlockSpec((chunk, half), lambda i, c: (c, 0)),
            pl.BlockSpec((1, RET_WIDTH), lambda i, c: (0, 0)),
        ],
        out_specs=[
            pl.BlockSpec((None, chunk, RET_WIDTH), lambda i, c: (i, c, 0)),
            state_spec,
        ],
        out_shape=[
            jax.ShapeDtypeStruct((b, l, RET_WIDTH), BF16),
            jax.ShapeDtypeStruct((b, RET_HEADS, RET_DK, RET_DV), F32),
        ],
        compiler_params=_compiler_params(("parallel", "arbitrary")),
        name="retention",
    )(proj3, proj3, proj3, proj3, s0, cos, sin, ret_norm_w.reshape(1, RET_WIDTH))


def _pool_kernel(u_ref, buf_ref, wp_ref, ps_ref, o_ref, nb_ref, xp_ref, *, bb, tl, start):
    t = pl.program_id(1)

    @pl.when(t == 0)
    def _():
        xp_ref[:, 0:1, :] = jnp.zeros((bb, 1, POOL_WIDTH), F32)
        xp_ref[:, 1:POOL_HIST, :] = buf_ref[...]

    xp_ref[:, POOL_HIST:POOL_HIST + tl, :] = u_ref[...]
    pos = (lax.broadcasted_iota(jnp.int32, (bb, tl, POOL_GC), 1) + t * tl).astype(F32) + start

    for g, w in enumerate(POOL_WINDOWS):
        cs = slice(g * POOL_GC, (g + 1) * POOL_GC)
        u = xp_ref[:, POOL_HIST:POOL_HIST + tl, cs]
        acc = u
        for j in range(1, w):
            acc = acc + xp_ref[:, POOL_HIST - j:POOL_HIST - j + tl, cs]
        cnt = jnp.minimum(pos + 1.0, float(w))
        pooled = (acc / cnt - u).astype(BF16).reshape(bb * tl, POOL_GC)
        y = jnp.dot(pooled, wp_ref[g].astype(BF16), preferred_element_type=F32)
        y = y * ps_ref[:, cs]
        o_ref[:, :, cs] = y.reshape(bb, tl, POOL_GC).astype(BF16)

    nb_ref[...] = xp_ref[:, tl + 1:tl + POOL_HIST, :]
    if tl >= POOL_HIST:
        xp_ref[:, 0:POOL_HIST, :] = xp_ref[:, tl:tl + POOL_HIST, :]


def _pool(proj3, buf, w_pool, pool_scale, start, *, bb, tl):
    b, l, _ = proj3.shape
    assert l == tl or tl >= POOL_HIST
    u_col = (proj3.shape[2] - POOL_WIDTH) // POOL_WIDTH
    return pl.pallas_call(
        functools.partial(_pool_kernel, bb=bb, tl=tl, start=float(start)),
        grid=(b // bb, l // tl),
        in_specs=[
            pl.BlockSpec((bb, tl, POOL_WIDTH), lambda i, t: (i, t, u_col)),
            pl.BlockSpec((bb, POOL_BUF, POOL_WIDTH), lambda i, t: (i, 0, 0)),
            pl.BlockSpec(w_pool.shape, lambda i, t: (0, 0, 0)),
            pl.BlockSpec((1, POOL_WIDTH), lambda i, t: (0, 0)),
        ],
        out_specs=[
            pl.BlockSpec((bb, tl, POOL_WIDTH), lambda i, t: (i, t, 0)),
            pl.BlockSpec((bb, POOL_BUF, POOL_WIDTH), lambda i, t: (i, 0, 0)),
        ],
        out_shape=[
            jax.ShapeDtypeStruct((b, l, POOL_WIDTH), BF16),
            jax.ShapeDtypeStruct((b, POOL_BUF, POOL_WIDTH), F32),
        ],
        scratch_shapes=[pltpu.VMEM((bb, POOL_HIST + tl, POOL_WIDTH), F32)],
        compiler_params=_compiler_params(("parallel", "arbitrary")),
        name="pool",
    )(proj3, buf, w_pool, pool_scale.reshape(1, POOL_WIDTH))


def _outproj_kernel(r_ref, p_ref, w_ref, x_ref, nw_ref, o_ref):
    mix = jnp.dot(r_ref[...], w_ref[0:RET_WIDTH, :], preferred_element_type=F32)
    mix = mix + jnp.dot(p_ref[...], w_ref[RET_WIDTH:, :], preferred_element_type=F32)
    o_ref[...] = x_ref[...] + _rms_rows(mix, nw_ref[...])


def _outproj(ret_out, pool_out, w_out_b, x2, norm_w, *, tm):
    m, d = x2.shape
    return pl.pallas_call(
        _outproj_kernel,
        grid=(m // tm,),
        in_specs=[
            pl.BlockSpec((tm, RET_WIDTH), lambda i: (i, 0)),
            pl.BlockSpec((tm, POOL_WIDTH), lambda i: (i, 0)),
            pl.BlockSpec(w_out_b.shape, lambda i: (0, 0)),
            pl.BlockSpec((tm, d), lambda i: (i, 0)),
            pl.BlockSpec((1, d), lambda i: (0, 0)),
        ],
        out_specs=pl.BlockSpec((tm, d), lambda i: (i, 0)),
        out_shape=jax.ShapeDtypeStruct((m, d), F32),
        compiler_params=_compiler_params(("parallel",)),
        name="outproj",
    )(ret_out, pool_out, w_out_b, x2, norm_w.reshape(1, d))


def _ffn_kernel(x_ref, nw1_ref, wg_ref, wu_ref, wd_ref, nw2_ref, o_ref, h_ref, *, tm, chunk):
    j = pl.program_id(1)

    @pl.when(j == 0)
    def _():
        _norm_to_scratch(x_ref, nw1_ref, h_ref, tm, chunk)

    h = h_ref[...]
    gate = jnp.dot(h, wg_ref[...], preferred_element_type=F32)
    up = jnp.dot(h, wu_ref[...], preferred_element_type=F32)
    act = (_silu(gate) * up).astype(BF16)
    part = jnp.dot(act, wd_ref[...], preferred_element_type=F32)

    @pl.when(j == 0)
    def _():
        o_ref[...] = part

    @pl.when(j > 0)
    def _():
        o_ref[...] += part

    @pl.when(j == pl.num_programs(1) - 1)
    def _():
        o_ref[...] = x_ref[...] + _rms_rows(o_ref[...], nw2_ref[...])


def _ffn(x1, norm_pre, wg_b, wu_b, wd_b, norm_post, *, tm, tf):
    m, d = x1.shape
    dff = wg_b.shape[1]
    return pl.pallas_call(
        functools.partial(_ffn_kernel, tm=tm, chunk=128),
        grid=(m // tm, dff // tf),
        in_specs=[
            pl.BlockSpec((tm, d), lambda i, j: (i, 0)),
            pl.BlockSpec((1, d), lambda i, j: (0, 0)),
            pl.BlockSpec((d, tf), lambda i, j: (0, j)),
            pl.BlockSpec((d, tf), lambda i, j: (0, j)),
            pl.BlockSpec((tf, d), lambda i, j: (j, 0)),
            pl.BlockSpec((1, d), lambda i, j: (0, 0)),
        ],
        out_specs=pl.BlockSpec((tm, d), lambda i, j: (i, 0)),
        out_shape=jax.ShapeDtypeStruct((m, d), F32),
        scratch_shapes=[pltpu.VMEM((tm, d), BF16)],
        compiler_params=_compiler_params(("parallel", "arbitrary")),
        name="ffn",
    )(x1, norm_pre.reshape(1, d), wg_b, wu_b, wd_b, norm_post.reshape(1, d))


def _rope_tables(length, start):
    pos = start + jnp.arange(length, dtype=F32)
    inv_freq = 1.0 / (ROPE_BASE ** (jnp.arange(0, RET_DK, 2, dtype=F32) / RET_DK))
    ang = pos[:, None] * inv_freq[None, :]
    return jnp.cos(ang), jnp.sin(ang)


def _layer(x, s0, pool_buf, start, weights, *, pool_bb, pool_tl):
    (norm_mix_pre, norm_mix_post, w_in, ret_norm_w, w_pool, pool_scale, w_out_b,
     norm_ffn_pre, norm_ffn_post, wg_b, wu_b, wd_b) = weights
    b, l, d = x.shape
    m = b * l
    x2 = x.reshape(m, d)
    proj3 = _inproj(x2, norm_mix_pre, w_in, tm=1024, tn=512).reshape(b, l, -1)
    cos, sin = _rope_tables(l, start)
    ret_out, s_new = _retention(proj3, s0, cos, sin, ret_norm_w)
    pool_out, new_buf = _pool(proj3, pool_buf, w_pool, pool_scale, start,
                              bb=pool_bb, tl=pool_tl)
    x1 = _outproj(ret_out.reshape(m, RET_WIDTH), pool_out.reshape(m, POOL_WIDTH),
                  w_out_b, x2, norm_mix_post, tm=512)
    y = _ffn(x1, norm_ffn_pre, wg_b, wu_b, wd_b, norm_ffn_post, tm=512, tf=512)
    return y.reshape(b, l, d), s_new, new_buf


def kernel(x_prompt, x_sample, state_ret, state_pool, norm_mix_pre, norm_mix_post, w_in,
           ret_norm_w, w_pool, pool_scale, w_out, norm_ffn_pre, norm_ffn_post,
           w_gate, w_up, w_down):
    weights = (norm_mix_pre, norm_mix_post, w_in, ret_norm_w, w_pool, pool_scale,
               w_out.astype(BF16), norm_ffn_pre, norm_ffn_post,
               w_gate.astype(BF16), w_up.astype(BF16), w_down.astype(BF16))
    bp = x_prompt.shape[0]
    s0_prompt = jnp.zeros((bp,) + state_ret.shape[1:], F32)
    buf_prompt = jnp.zeros((bp,) + state_pool.shape[1:], x_prompt.dtype)
    yp, sp, new_bp = _layer(x_prompt, s0_prompt, buf_prompt, 0.0, weights,
                            pool_bb=1, pool_tl=256)
    ys, ss, new_bs = _layer(x_sample, state_ret, state_pool, float(PAST_LEN), weights,
                            pool_bb=16, pool_tl=x_sample.shape[1])
    return (yp, ys, sp.astype(x_prompt.dtype), new_bp.astype(x_prompt.dtype),
            ss.astype(state_ret.dtype), new_bs.astype(state_pool.dtype))
```

```python
import functools
import math

import jax
import jax.numpy as jnp
from jax import lax
from jax.experimental import pallas as pl
from jax.experimental.pallas import tpu as pltpu

F32 = jnp.float32
BF16 = jnp.bfloat16

RET_HEADS = 4
RET_DK = 256
RET_DV = 256
RET_WIDTH = RET_HEADS * RET_DV
RET_CHUNK = 128
ROPE_BASE = 10000.0
POOL_WINDOWS = (2, 4, 8, 16)
POOL_GC = 256
POOL_WIDTH = POOL_GC * len(POOL_WINDOWS)
POOL_BUF = max(POOL_WINDOWS) - 1
POOL_HIST = POOL_BUF + 1
PAST_LEN = 16384
EPS = 1e-6

VMEM_LIMIT_BYTES = 56 * 1024 * 1024


def _compiler_params(semantics):
    return pltpu.CompilerParams(dimension_semantics=semantics,
                                vmem_limit_bytes=VMEM_LIMIT_BYTES)


def _rms_rows(x, w):
    ms = jnp.mean(x * x, axis=-1, keepdims=True)
    return (x * lax.rsqrt(ms + EPS)) * w


def _silu(g):
    return g * (1.0 / (1.0 + jnp.exp(-g)))


def _norm_to_scratch(x_ref, w_ref, h_ref, rows, chunk):
    def body(r, carry):
        sl = pl.ds(pl.multiple_of(r * chunk, chunk), chunk)
        h_ref[sl, :] = _rms_rows(x_ref[sl, :], w_ref[...]).astype(BF16)
        return carry
    lax.fori_loop(0, rows // chunk, body, 0)


def _inproj_kernel(x_ref, nw_ref, w_ref, o_ref, u_ref, h_ref, *, tm, chunk, n_main):
    j = pl.program_id(1)

    @pl.when(j == 0)
    def _():
        _norm_to_scratch(x_ref, nw_ref, h_ref, tm, chunk)

    acc = jnp.dot(h_ref[...], w_ref[...], preferred_element_type=F32)

    @pl.when(j < n_main)
    def _():
        o_ref[...] = acc.astype(BF16)

    @pl.when(j >= n_main)
    def _():
        u_ref[...] = acc


def _inproj(x2, norm_w, w_in_b, *, tm, tn):
    m, d = x2.shape
    n = w_in_b.shape[1]
    n_main = (n - POOL_WIDTH) // tn
    return pl.pallas_call(
        functools.partial(_inproj_kernel, tm=tm, chunk=128, n_main=n_main),
        grid=(m // tm, n // tn),
        in_specs=[
            pl.BlockSpec((tm, d), lambda i, j: (i, 0)),
            pl.BlockSpec((1, d), lambda i, j: (0, 0)),
            pl.BlockSpec((d, tn), lambda i, j: (0, j)),
        ],
        out_specs=[
            pl.BlockSpec((tm, tn), lambda i, j: (i, jnp.minimum(j, n_main - 1))),
            pl.BlockSpec((tm, tn), lambda i, j: (i, jnp.maximum(j - n_main, 0))),
        ],
        out_shape=[
            jax.ShapeDtypeStruct((m, n - POOL_WIDTH), BF16),
            jax.ShapeDtypeStruct((m, POOL_WIDTH), F32),
        ],
        scratch_shapes=[pltpu.VMEM((tm, d), BF16)],
        compiler_params=_compiler_params(("parallel", "arbitrary")),
        name="inproj",
    )(x2, norm_w.reshape(1, d), w_in_b)


def _rope(x, cos, sin):
    half = x.shape[-1] // 2
    x1, x2 = x[:, :half], x[:, half:]
    return jnp.concatenate([x1 * cos - x2 * sin, x2 * cos + x1 * sin], axis=-1)


def _retention_kernel(q_ref, k_ref, v_ref, g_ref, s0_ref, cos_ref, sin_ref, nw_ref,
                      o_ref, s_ref, *, bb, chunk, log_gammas):
    @pl.when(pl.program_id(1) == 0)
    def _():
        s_ref[...] = s0_ref[...]

    cos = cos_ref[...]
    sin = sin_ref[...]
    row = lax.broadcasted_iota(jnp.int32, (chunk, chunk), 0)
    col = lax.broadcasted_iota(jnp.int32, (chunk, chunk), 1)
    diff = (row - col).astype(F32)
    pos = lax.broadcasted_iota(jnp.int32, (chunk, 1), 0).astype(F32)

    for h, lg in enumerate(log_gammas):
        cs = slice(h * RET_DK, (h + 1) * RET_DK)
        decay = jnp.where(diff >= 0, jnp.exp(jnp.maximum(diff, 0.0) * lg), 0.0)
        q_dec = jnp.exp((pos + 1.0) * lg)
        k_dec = jnp.exp((chunk - 1.0 - pos) * lg)
        for bi in range(bb):
            qr = _rope(q_ref[bi, :, cs].astype(F32), cos, sin)
            kr = _rope(k_ref[bi, :, cs].astype(F32), cos, sin) * (RET_DK ** -0.5)
            vb = v_ref[bi, :, cs]
            s_old = s_ref[bi, h]

            qb = qr.astype(BF16)
            scores = lax.dot_general(qb, kr.astype(BF16), (((1,), (1,)), ((), ())),
                                     preferred_element_type=F32) * decay
            o_inner = jnp.dot(scores.astype(BF16), vb, preferred_element_type=F32)
            o_cross = jnp.dot(qb, s_old.astype(BF16), preferred_element_type=F32) * q_dec
            kd = (kr * k_dec).astype(BF16)
            s_ref[bi, h] = math.exp(chunk * lg) * s_old + lax.dot_general(
                kd, vb, (((0,), (0,)), ((), ())), preferred_element_type=F32)

            o = o_inner + o_cross
            o = o * lax.rsqrt(jnp.mean(o * o, axis=-1, keepdims=True) + EPS)
            o = o * nw_ref[:, cs]
            o_ref[bi, :, cs] = (_silu(g_ref[bi, :, cs].astype(F32)) * o).astype(BF16)


def _retention(qkvg3, s0, cos, sin, ret_norm_w, *, bb):
    b, l, _ = qkvg3.shape
    chunk = RET_CHUNK if l % RET_CHUNK == 0 else l
    log_gammas = tuple(math.log(1.0 - 2.0 ** (-5.0 - h)) for h in range(RET_HEADS))
    half = RET_DK // 2

    def col_block(idx):
        return pl.BlockSpec((bb, chunk, RET_WIDTH), lambda i, c: (i, c, idx))

    state_spec = pl.BlockSpec((bb, RET_HEADS, RET_DK, RET_DV), lambda i, c: (i, 0, 0, 0))
    return pl.pallas_call(
        functools.partial(_retention_kernel, bb=bb, chunk=chunk, log_gammas=log_gammas),
        grid=(b // bb, l // chunk),
        in_specs=[
            col_block(0), col_block(1), col_block(2), col_block(3),
            state_spec,
            pl.BlockSpec((chunk, half), lambda i, c: (c, 0)),
            pl.BlockSpec((chunk, half), lambda i, c: (c, 0)),
            pl.BlockSpec((1, RET_WIDTH), lambda i, c: (0, 0)),
        ],
        out_specs=[
            pl.BlockSpec((bb, chunk, RET_WIDTH), lambda i, c: (i, c, 0)),
            state_spec,
        ],
        out_shape=[
            jax.ShapeDtypeStruct((b, l, RET_WIDTH), BF16),
            jax.ShapeDtypeStruct((b, RET_HEADS, RET_DK, RET_DV), F32),
        ],
        compiler_params=_compiler_params(("parallel", "arbitrary")),
        name="retention",
    )(qkvg3, qkvg3, qkvg3, qkvg3, s0, cos, sin, ret_norm_w.reshape(1, RET_WIDTH))


def _pool_kernel(u_ref, buf_ref, wp_ref, ps_ref, o_ref, nb_ref, xp_ref, *, bb, tl, start):
    t = pl.program_id(1)

    @pl.when(t == 0)
    def _():
        xp_ref[:, 0:1, :] = jnp.zeros((bb, 1, POOL_WIDTH), F32)
        xp_ref[:, 1:POOL_HIST, :] = buf_ref[...]

    xp_ref[:, POOL_HIST:POOL_HIST + tl, :] = u_ref[...]
    pos = (lax.broadcasted_iota(jnp.int32, (bb, tl, POOL_GC), 1) + t * tl).astype(F32) + start

    for g, w in enumerate(POOL_WINDOWS):
        cs = slice(g * POOL_GC, (g + 1) * POOL_GC)
        u = xp_ref[:, POOL_HIST:POOL_HIST + tl, cs]
        acc = u
        for j in range(1, w):
            acc = acc + xp_ref[:, POOL_HIST - j:POOL_HIST - j + tl, cs]
        cnt = jnp.minimum(pos + 1.0, float(w))
        pooled = (acc / cnt - u).astype(BF16).reshape(bb * tl, POOL_GC)
        y = jnp.dot(pooled, wp_ref[g].astype(BF16), preferred_element_type=F32)
        y = y * ps_ref[:, cs]
        o_ref[:, :, cs] = y.reshape(bb, tl, POOL_GC).astype(BF16)

    nb_ref[...] = xp_ref[:, tl + 1:tl + POOL_HIST, :]
    if tl >= POOL_HIST:
        xp_ref[:, 0:POOL_HIST, :] = xp_ref[:, tl:tl + POOL_HIST, :]


def _pool(u3, buf, w_pool, pool_scale, start, *, bb, tl):
    b, l, _ = u3.shape
    assert l == tl or tl >= POOL_HIST
    return pl.pallas_call(
        functools.partial(_pool_kernel, bb=bb, tl=tl, start=float(start)),
        grid=(b // bb, l // tl),
        in_specs=[
            pl.BlockSpec((bb, tl, POOL_WIDTH), lambda i, t: (i, t, 0)),
            pl.BlockSpec((bb, POOL_BUF, POOL_WIDTH), lambda i, t: (i, 0, 0)),
            pl.BlockSpec(w_pool.shape, lambda i, t: (0, 0, 0)),
            pl.BlockSpec((1, POOL_WIDTH), lambda i, t: (0, 0)),
        ],
        out_specs=[
            pl.BlockSpec((bb, tl, POOL_WIDTH), lambda i, t: (i, t, 0)),
            pl.BlockSpec((bb, POOL_BUF, POOL_WIDTH), lambda i, t: (i, 0, 0)),
        ],
        out_shape=[
            jax.ShapeDtypeStruct((b, l, POOL_WIDTH), BF16),
            jax.ShapeDtypeStruct((b, POOL_BUF, POOL_WIDTH), F32),
        ],
        scratch_shapes=[pltpu.VMEM((bb, POOL_HIST + tl, POOL_WIDTH), F32)],
        compiler_params=_compiler_params(("parallel", "arbitrary")),
        name="pool",
    )(u3, buf, w_pool, pool_scale.reshape(1, POOL_WIDTH))


def _outproj_kernel(r_ref, p_ref, w_ref, x_ref, nw_ref, o_ref):
    mix = jnp.dot(r_ref[...], w_ref[0:RET_WIDTH, :], preferred_element_type=F32)
    mix = mix + jnp.dot(p_ref[...], w_ref[RET_WIDTH:, :], preferred_element_type=F32)
    o_ref[...] = x_ref[...] + _rms_rows(mix, nw_ref[...])


def _outproj(ret_out, pool_out, w_out_b, x2, norm_w, *, tm):
    m, d = x2.shape
    return pl.pallas_call(
        _outproj_kernel,
        grid=(m // tm,),
        in_specs=[
            pl.BlockSpec((tm, RET_WIDTH), lambda i: (i, 0)),
            pl.BlockSpec((tm, POOL_WIDTH), lambda i: (i, 0)),
            pl.BlockSpec(w_out_b.shape, lambda i: (0, 0)),
            pl.BlockSpec((tm, d), lambda i: (i, 0)),
            pl.BlockSpec((1, d), lambda i: (0, 0)),
        ],
        out_specs=pl.BlockSpec((tm, d), lambda i: (i, 0)),
        out_shape=jax.ShapeDtypeStruct((m, d), F32),
        compiler_params=_compiler_params(("parallel",)),
        name="outproj",
    )(ret_out, pool_out, w_out_b, x2, norm_w.reshape(1, d))


def _ffn_kernel(x_ref, nw1_ref, wg_ref, wu_ref, wd_ref, nw2_ref, o_ref, h_ref, *, tm, chunk):
    j = pl.program_id(1)

    @pl.when(j == 0)
    def _():
        _norm_to_scratch(x_ref, nw1_ref, h_ref, tm, chunk)

    h = h_ref[...]
    gate = jnp.dot(h, wg_ref[...], preferred_element_type=F32)
    up = jnp.dot(h, wu_ref[...], preferred_element_type=F32)
    act = (_silu(gate) * up).astype(BF16)
    part = jnp.dot(act, wd_ref[...], preferred_element_type=F32)

    @pl.when(j == 0)
    def _():
        o_ref[...] = part

    @pl.when(j > 0)
    def _():
        o_ref[...] += part

    @pl.when(j == pl.num_programs(1) - 1)
    def _():
        o_ref[...] = x_ref[...] + _rms_rows(o_ref[...], nw2_ref[...])


def _ffn(x1, norm_pre, wg_b, wu_b, wd_b, norm_post, *, tm, tf):
    m, d = x1.shape
    dff = wg_b.shape[1]
    return pl.pallas_call(
        functools.partial(_ffn_kernel, tm=tm, chunk=128),
        grid=(m // tm, dff // tf),
        in_specs=[
            pl.BlockSpec((tm, d), lambda i, j: (i, 0)),
            pl.BlockSpec((1, d), lambda i, j: (0, 0)),
            pl.BlockSpec((d, tf), lambda i, j: (0, j)),
            pl.BlockSpec((d, tf), lambda i, j: (0, j)),
            pl.BlockSpec((tf, d), lambda i, j: (j, 0)),
            pl.BlockSpec((1, d), lambda i, j: (0, 0)),
        ],
        out_specs=pl.BlockSpec((tm, d), lambda i, j: (i, 0)),
        out_shape=jax.ShapeDtypeStruct((m, d), F32),
        scratch_shapes=[pltpu.VMEM((tm, d), BF16)],
        compiler_params=_compiler_params(("parallel", "arbitrary")),
        name="ffn",
    )(x1, norm_pre.reshape(1, d), wg_b, wu_b, wd_b, norm_post.reshape(1, d))


def _rope_tables(length, start):
    pos = start + jnp.arange(length, dtype=F32)
    inv_freq = 1.0 / (ROPE_BASE ** (jnp.arange(0, RET_DK, 2, dtype=F32) / RET_DK))
    ang = pos[:, None] * inv_freq[None, :]
    return jnp.cos(ang), jnp.sin(ang)


def _layer(x, s0, pool_buf, start, weights, *, ret_bb, pool_bb, pool_tl):
    (norm_mix_pre, norm_mix_post, w_in_b, ret_norm_w, w_pool, pool_scale, w_out_b,
     norm_ffn_pre, norm_ffn_post, wg_b, wu_b, wd_b) = weights
    b, l, d = x.shape
    m = b * l
    x2 = x.reshape(m, d)
    qkvg, u = _inproj(x2, norm_mix_pre, w_in_b, tm=1024, tn=512)
    cos, sin = _rope_tables(l, start)
    ret_out, s_new = _retention(qkvg.reshape(b, l, -1), s0, cos, sin, ret_norm_w, bb=ret_bb)
    pool_out, new_buf = _pool(u.reshape(b, l, -1), pool_buf, w_pool, pool_scale, start,
                              bb=pool_bb, tl=pool_tl)
    x1 = _outproj(ret_out.reshape(m, RET_WIDTH), pool_out.reshape(m, POOL_WIDTH),
                  w_out_b, x2, norm_mix_post, tm=512)
    y = _ffn(x1, norm_ffn_pre, wg_b, wu_b, wd_b, norm_ffn_post, tm=512, tf=512)
    return y.reshape(b, l, d), s_new, new_buf


def kernel(x_prompt, x_sample, state_ret, state_pool, norm_mix_pre, norm_mix_post, w_in,
           ret_norm_w, w_pool, pool_scale, w_out, norm_ffn_pre, norm_ffn_post,
           w_gate, w_up, w_down):
    weights = (norm_mix_pre, norm_mix_post, w_in.astype(BF16), ret_norm_w, w_pool, pool_scale,
               w_out.astype(BF16), norm_ffn_pre, norm_ffn_post,
               w_gate.astype(BF16), w_up.astype(BF16), w_down.astype(BF16))
    bp = x_prompt.shape[0]
    s0_prompt = jnp.zeros((bp,) + state_ret.shape[1:], F32)
    buf_prompt = jnp.zeros((bp,) + state_pool.shape[1:], x_prompt.dtype)
    yp, sp, new_bp = _layer(x_prompt, s0_prompt, buf_prompt, 0.0, weights,
                            ret_bb=1, pool_bb=1, pool_tl=256)
    ys, ss, new_bs = _layer(x_sample, state_ret, state_pool, float(PAST_LEN), weights,
                            ret_bb=4, pool_bb=16, pool_tl=x_sample.shape[1])
    return (yp, ys, sp.astype(x_prompt.dtype), new_bp.astype(x_prompt.dtype),
            ss.astype(state_ret.dtype), new_bs.astype(state_pool.dtype))
```

```python
import functools
import math

import jax
import jax.numpy as jnp
from jax import lax
from jax.experimental import pallas as pl
from jax.experimental.pallas import tpu as pltpu

F32 = jnp.float32
BF16 = jnp.bfloat16

RET_HEADS = 4
RET_DK = 256
RET_DV = 256
RET_WIDTH = RET_HEADS * RET_DV
RET_CHUNK = 128
ROPE_BASE = 10000.0
POOL_WINDOWS = (2, 4, 8, 16)
POOL_GC = 256
POOL_WIDTH = POOL_GC * len(POOL_WINDOWS)
POOL_BUF = max(POOL_WINDOWS) - 1
POOL_HIST = POOL_BUF + 1
PAST_LEN = 16384
EPS = 1e-6

VMEM_LIMIT_BYTES = 56 * 1024 * 1024


def _compiler_params(semantics):
    return pltpu.CompilerParams(dimension_semantics=semantics,
                                vmem_limit_bytes=VMEM_LIMIT_BYTES)


def _rms_rows(x, w):
    ms = jnp.mean(x * x, axis=-1, keepdims=True)
    return (x * lax.rsqrt(ms + EPS)) * w


def _silu(g):
    return g * (1.0 / (1.0 + jnp.exp(-g)))


def _norm_to_scratch(x_ref, w_ref, h_ref, rows, chunk):
    def body(r, carry):
        sl = pl.ds(pl.multiple_of(r * chunk, chunk), chunk)
        h_ref[sl, :] = _rms_rows(x_ref[sl, :], w_ref[...]).astype(BF16)
        return carry
    lax.fori_loop(0, rows // chunk, body, 0)


def _inproj_kernel(x_ref, nw_ref, w_ref, o_ref, u_ref, h_ref, *, tm, chunk, n_main):
    j = pl.program_id(1)

    @pl.when(j == 0)
    def _():
        _norm_to_scratch(x_ref, nw_ref, h_ref, tm, chunk)

    acc = jnp.dot(h_ref[...], w_ref[...], preferred_element_type=F32)

    @pl.when(j < n_main)
    def _():
        o_ref[...] = acc.astype(BF16)

    @pl.when(j >= n_main)
    def _():
        u_ref[...] = acc


def _inproj(x2, norm_w, w_in_t, *, tm):
    m, d = x2.shape
    n_tiles, _, tn = w_in_t.shape
    n = n_tiles * tn
    n_main = (n - POOL_WIDTH) // tn
    return pl.pallas_call(
        functools.partial(_inproj_kernel, tm=tm, chunk=128, n_main=n_main),
        grid=(m // tm, n_tiles),
        in_specs=[
            pl.BlockSpec((tm, d), lambda i, j: (i, 0)),
            pl.BlockSpec((1, d), lambda i, j: (0, 0)),
            pl.BlockSpec((None, d, tn), lambda i, j: (j, 0, 0)),
        ],
        out_specs=[
            pl.BlockSpec((tm, tn), lambda i, j: (i, jnp.minimum(j, n_main - 1))),
            pl.BlockSpec((tm, tn), lambda i, j: (i, jnp.maximum(j - n_main, 0))),
        ],
        out_shape=[
            jax.ShapeDtypeStruct((m, n - POOL_WIDTH), BF16),
            jax.ShapeDtypeStruct((m, POOL_WIDTH), F32),
        ],
        scratch_shapes=[pltpu.VMEM((tm, d), BF16)],
        compiler_params=_compiler_params(("parallel", "arbitrary")),
        name="inproj",
    )(x2, norm_w.reshape(1, d), w_in_t)


def _rope(x, cos, sin):
    half = x.shape[-1] // 2
    x1, x2 = x[:, :half], x[:, half:]
    return jnp.concatenate([x1 * cos - x2 * sin, x2 * cos + x1 * sin], axis=-1)


def _retention_kernel(q_ref, k_ref, v_ref, g_ref, s0_ref, cos_ref, sin_ref, nw_ref,
                      o_ref, s_ref, *, bb, chunk, log_gammas):
    @pl.when(pl.program_id(1) == 0)
    def _():
        s_ref[...] = s0_ref[...]

    cos = cos_ref[...]
    sin = sin_ref[...]
    row = lax.broadcasted_iota(jnp.int32, (chunk, chunk), 0)
    col = lax.broadcasted_iota(jnp.int32, (chunk, chunk), 1)
    diff = (row - col).astype(F32)
    pos = lax.broadcasted_iota(jnp.int32, (chunk, 1), 0).astype(F32)

    for h, lg in enumerate(log_gammas):
        cs = slice(h * RET_DK, (h + 1) * RET_DK)
        decay = jnp.where(diff >= 0, jnp.exp(jnp.maximum(diff, 0.0) * lg), 0.0)
        q_dec = jnp.exp((pos + 1.0) * lg)
        k_dec = jnp.exp((chunk - 1.0 - pos) * lg)
        for bi in range(bb):
            qr = _rope(q_ref[bi, :, cs].astype(F32), cos, sin)
            kr = _rope(k_ref[bi, :, cs].astype(F32), cos, sin) * (RET_DK ** -0.5)
            vb = v_ref[bi, :, cs]
            s_old = s_ref[bi, h]

            qb = qr.astype(BF16)
            scores = lax.dot_general(qb, kr.astype(BF16), (((1,), (1,)), ((), ())),
                                     preferred_element_type=F32) * decay
            o_inner = jnp.dot(scores.astype(BF16), vb, preferred_element_type=F32)
            o_cross = jnp.dot(qb, s_old.astype(BF16), preferred_element_type=F32) * q_dec
            kd = (kr * k_dec).astype(BF16)
            s_ref[bi, h] = math.exp(chunk * lg) * s_old + lax.dot_general(
                kd, vb, (((0,), (0,)), ((), ())), preferred_element_type=F32)

            o = o_inner + o_cross
            o = o * lax.rsqrt(jnp.mean(o * o, axis=-1, keepdims=True) + EPS)
            o = o * nw_ref[:, cs]
            o_ref[bi, :, cs] = (_silu(g_ref[bi, :, cs].astype(F32)) * o).astype(BF16)


def _retention(qkvg3, s0, cos, sin, ret_norm_w, *, bb):
    b, l, _ = qkvg3.shape
    chunk = RET_CHUNK if l % RET_CHUNK == 0 else l
    log_gammas = tuple(math.log(1.0 - 2.0 ** (-5.0 - h)) for h in range(RET_HEADS))
    half = RET_DK // 2

    def col_block(idx):
        return pl.BlockSpec((bb, chunk, RET_WIDTH), lambda i, c: (i, c, idx))

    state_spec = pl.BlockSpec((bb, RET_HEADS, RET_DK, RET_DV), lambda i, c: (i, 0, 0, 0))
    return pl.pallas_call(
        functools.partial(_retention_kernel, bb=bb, chunk=chunk, log_gammas=log_gammas),
        grid=(b // bb, l // chunk),
        in_specs=[
            col_block(0), col_block(1), col_block(2), col_block(3),
            state_spec,
            pl.BlockSpec((chunk, half), lambda i, c: (c, 0)),
            pl.BlockSpec((chunk, half), lambda i, c: (c, 0)),
            pl.BlockSpec((1, RET_WIDTH), lambda i, c: (0, 0)),
        ],
        out_specs=[
            pl.BlockSpec((bb, chunk, RET_WIDTH), lambda i, c: (i, c, 0)),
            state_spec,
        ],
        out_shape=[
            jax.ShapeDtypeStruct((b, l, RET_WIDTH), BF16),
            jax.ShapeDtypeStruct((b, RET_HEADS, RET_DK, RET_DV), F32),
        ],
        compiler_params=_compiler_params(("parallel", "arbitrary")),
        name="retention",
    )(qkvg3, qkvg3, qkvg3, qkvg3, s0, cos, sin, ret_norm_w.reshape(1, RET_WIDTH))


def _pool_kernel(u_ref, buf_ref, wp_ref, ps_ref, o_ref, nb_ref, xp_ref, *, bb, tl, start):
    t = pl.program_id(1)

    @pl.when(t == 0)
    def _():
        xp_ref[:, 0:1, :] = jnp.zeros((bb, 1, POOL_WIDTH), F32)
        xp_ref[:, 1:POOL_HIST, :] = buf_ref[...]

    xp_ref[:, POOL_HIST:POOL_HIST + tl, :] = u_ref[...]
    pos = (lax.broadcasted_iota(jnp.int32, (bb, tl, POOL_GC), 1) + t * tl).astype(F32) + start

    for g, w in enumerate(POOL_WINDOWS):
        cs = slice(g * POOL_GC, (g + 1) * POOL_GC)
        u = xp_ref[:, POOL_HIST:POOL_HIST + tl, cs]
        acc = u
        for j in range(1, w):
            acc = acc + xp_ref[:, POOL_HIST - j:POOL_HIST - j + tl, cs]
        cnt = jnp.minimum(pos + 1.0, float(w))
        pooled = (acc / cnt - u).astype(BF16).reshape(bb * tl, POOL_GC)
        y = jnp.dot(pooled, wp_ref[g].astype(BF16), preferred_element_type=F32)
        y = y * ps_ref[:, cs]
        o_ref[:, :, cs] = y.reshape(bb, tl, POOL_GC).astype(BF16)

    nb_ref[...] = xp_ref[:, tl + 1:tl + POOL_HIST, :]
    if tl >= POOL_HIST:
        xp_ref[:, 0:POOL_HIST, :] = xp_ref[:, tl:tl + POOL_HIST, :]


def _pool(u3, buf, w_pool, pool_scale, start, *, bb, tl):
    b, l, _ = u3.shape
    assert l == tl or tl >= POOL_HIST
    return pl.pallas_call(
        functools.partial(_pool_kernel, bb=bb, tl=tl, start=float(start)),
        grid=(b // bb, l // tl),
        in_specs=[
            pl.BlockSpec((bb, tl, POOL_WIDTH), lambda i, t: (i, t, 0)),
            pl.BlockSpec((bb, POOL_BUF, POOL_WIDTH), lambda i, t: (i, 0, 0)),
            pl.BlockSpec(w_pool.shape, lambda i, t: (0, 0, 0)),
            pl.BlockSpec((1, POOL_WIDTH), lambda i, t: (0, 0)),
        ],
        out_specs=[
            pl.BlockSpec((bb, tl, POOL_WIDTH), lambda i, t: (i, t, 0)),
            pl.BlockSpec((bb, POOL_BUF, POOL_WIDTH), lambda i, t: (i, 0, 0)),
        ],
        out_shape=[
            jax.ShapeDtypeStruct((b, l, POOL_WIDTH), BF16),
            jax.ShapeDtypeStruct((b, POOL_BUF, POOL_WIDTH), F32),
        ],
        scratch_shapes=[pltpu.VMEM((bb, POOL_HIST + tl, POOL_WIDTH), F32)],
        compiler_params=_compiler_params(("parallel", "arbitrary")),
        name="pool",
    )(u3, buf, w_pool, pool_scale.reshape(1, POOL_WIDTH))


def _outproj_kernel(r_ref, p_ref, w_ref, x_ref, nw_ref, o_ref):
    mix = jnp.dot(r_ref[...], w_ref[0:RET_WIDTH, :], preferred_element_type=F32)
    mix = mix + jnp.dot(p_ref[...], w_ref[RET_WIDTH:, :], preferred_element_type=F32)
    o_ref[...] = x_ref[...] + _rms_rows(mix, nw_ref[...])


def _outproj(ret_out, pool_out, w_out_b, x2, norm_w, *, tm):
    m, d = x2.shape
    return pl.pallas_call(
        _outproj_kernel,
        grid=(m // tm,),
        in_specs=[
            pl.BlockSpec((tm, RET_WIDTH), lambda i: (i, 0)),
            pl.BlockSpec((tm, POOL_WIDTH), lambda i: (i, 0)),
            pl.BlockSpec(w_out_b.shape, lambda i: (0, 0)),
            pl.BlockSpec((tm, d), lambda i: (i, 0)),
            pl.BlockSpec((1, d), lambda i: (0, 0)),
        ],
        out_specs=pl.BlockSpec((tm, d), lambda i: (i, 0)),
        out_shape=jax.ShapeDtypeStruct((m, d), F32),
        compiler_params=_compiler_params(("parallel",)),
        name="outproj",
    )(ret_out, pool_out, w_out_b, x2, norm_w.reshape(1, d))


def _ffn_kernel(x_ref, nw1_ref, wg_ref, wu_ref, wd_ref, nw2_ref, o_ref, h_ref,
                *, tm, chunk, tf, sub):
    j = pl.program_id(1)

    @pl.when(j == 0)
    def _():
        _norm_to_scratch(x_ref, nw1_ref, h_ref, tm, chunk)
        o_ref[...] = jnp.zeros(o_ref.shape, F32)

    h = h_ref[...]
    part = None
    for s in range(tf // sub):
        cs = slice(s * sub, (s + 1) * sub)
        gate = jnp.dot(h, wg_ref[:, cs], preferred_element_type=F32)
        up = jnp.dot(h, wu_ref[:, cs], preferred_element_type=F32)
        act = (_silu(gate) * up).astype(BF16)
        p = jnp.dot(act, wd_ref[cs, :], preferred_element_type=F32)
        part = p if part is None else part + p
    o_ref[...] += part

    @pl.when(j == pl.num_programs(1) - 1)
    def _():
        o_ref[...] = x_ref[...] + _rms_rows(o_ref[...], nw2_ref[...])


def _ffn(x1, norm_pre, wg_t, wu_t, wd_b, norm_post, *, tm):
    m, d = x1.shape
    n_tiles, _, tf = wg_t.shape
    return pl.pallas_call(
        functools.partial(_ffn_kernel, tm=tm, chunk=128, tf=tf, sub=256),
        grid=(m // tm, n_tiles),
        in_specs=[
            pl.BlockSpec((tm, d), lambda i, j: (i, 0)),
            pl.BlockSpec((1, d), lambda i, j: (0, 0)),
            pl.BlockSpec((None, d, tf), lambda i, j: (j, 0, 0)),
            pl.BlockSpec((None, d, tf), lambda i, j: (j, 0, 0)),
            pl.BlockSpec((tf, d), lambda i, j: (j, 0)),
            pl.BlockSpec((1, d), lambda i, j: (0, 0)),
        ],
        out_specs=pl.BlockSpec((tm, d), lambda i, j: (i, 0)),
        out_shape=jax.ShapeDtypeStruct((m, d), F32),
        scratch_shapes=[pltpu.VMEM((tm, d), BF16)],
        compiler_params=_compiler_params(("parallel", "arbitrary")),
        name="ffn",
    )(x1, norm_pre.reshape(1, d), wg_t, wu_t, wd_b, norm_post.reshape(1, d))


def _rope_tables(length, start):
    pos = start + jnp.arange(length, dtype=F32)
    inv_freq = 1.0 / (ROPE_BASE ** (jnp.arange(0, RET_DK, 2, dtype=F32) / RET_DK))
    ang = pos[:, None] * inv_freq[None, :]
    return jnp.cos(ang), jnp.sin(ang)


def _column_tiles_bf16(w, tn):
    k, n = w.shape
    return w.astype(BF16).reshape(k, n // tn, tn).transpose(1, 0, 2)


def _layer(x, s0, pool_buf, start, weights, *, ret_bb, pool_bb, pool_tl):
    (norm_mix_pre, norm_mix_post, w_in_t, ret_norm_w, w_pool, pool_scale, w_out_b,
     norm_ffn_pre, norm_ffn_post, wg_t, wu_t, wd_b) = weights
    b, l, d = x.shape
    m = b * l
    x2 = x.reshape(m, d)
    qkvg, u = _inproj(x2, norm_mix_pre, w_in_t, tm=1024)
    cos, sin = _rope_tables(l, start)
    ret_out, s_new = _retention(qkvg.reshape(b, l, -1), s0, cos, sin, ret_norm_w, bb=ret_bb)
    pool_out, new_buf = _pool(u.reshape(b, l, -1), pool_buf, w_pool, pool_scale, start,
                              bb=pool_bb, tl=pool_tl)
    x1 = _outproj(ret_out.reshape(m, RET_WIDTH), pool_out.reshape(m, POOL_WIDTH),
                  w_out_b, x2, norm_mix_post, tm=512)
    y = _ffn(x1, norm_ffn_pre, wg_t, wu_t, wd_b, norm_ffn_post, tm=512)
    return y.reshape(b, l, d), s_new, new_buf


def kernel(x_prompt, x_sample, state_ret, state_pool, norm_mix_pre, norm_mix_post, w_in,
           ret_norm_w, w_pool, pool_scale, w_out, norm_ffn_pre, norm_ffn_post,
           w_gate, w_up, w_down):
    weights = (norm_mix_pre, norm_mix_post, _column_tiles_bf16(w_in, 512), ret_norm_w,
               w_pool, pool_scale, w_out.astype(BF16), norm_ffn_pre, norm_ffn_post,
               _column_tiles_bf16(w_gate, 512), _column_tiles_bf16(w_up, 512),
               w_down.astype(BF16))
    bp = x_prompt.shape[0]
    s0_prompt = jnp.zeros((bp,) + state_ret.shape[1:], F32)
    buf_prompt = jnp.zeros((bp,) + state_pool.shape[1:], x_prompt.dtype)
    yp, sp, new_bp = _layer(x_prompt, s0_prompt, buf_prompt, 0.0, weights,
                            ret_bb=1, pool_bb=1, pool_tl=256)
    ys, ss, new_bs = _layer(x_sample, state_ret, state_pool, float(PAST_LEN), weights,
                            ret_bb=4, pool_bb=16, pool_tl=x_sample.shape[1])
    return (yp, ys, sp.astype(x_prompt.dtype), new_bp.astype(x_prompt.dtype),
            ss.astype(state_ret.dtype), new_bs.astype(state_pool.dtype))
```

```python
import functools
import math

import jax
import jax.numpy as jnp
from jax import lax
from jax.experimental import pallas as pl
from jax.experimental.pallas import tpu as pltpu

F32 = jnp.float32
BF16 = jnp.bfloat16

RET_HEADS = 4
RET_DK = 256
RET_DV = 256
RET_WIDTH = RET_HEADS * RET_DV
RET_CHUNK = 128
ROPE_BASE = 10000.0
POOL_WINDOWS = (2, 4, 8, 16)
POOL_GC = 256
POOL_WIDTH = POOL_GC * len(POOL_WINDOWS)
POOL_BUF = max(POOL_WINDOWS) - 1
POOL_HIST = POOL_BUF + 1
PAST_LEN = 16384
EPS = 1e-6

VMEM_LIMIT_BYTES = 56 * 1024 * 1024


def _compiler_params(semantics):
    return pltpu.CompilerParams(dimension_semantics=semantics,
                                vmem_limit_bytes=VMEM_LIMIT_BYTES)


def _rms_rows(x, w):
    ms = jnp.mean(x * x, axis=-1, keepdims=True)
    return (x * lax.rsqrt(ms + EPS)) * w


def _silu(g):
    return g * (1.0 / (1.0 + jnp.exp(-g)))


def _norm_to_scratch(x_ref, w_ref, h_ref, rows, chunk):
    def body(r, carry):
        sl = pl.ds(pl.multiple_of(r * chunk, chunk), chunk)
        h_ref[sl, :] = _rms_rows(x_ref[sl, :], w_ref[...]).astype(BF16)
        return carry
    lax.fori_loop(0, rows // chunk, body, 0)


def _inproj_kernel(x_ref, nw_ref, w_ref, o_ref, u_ref, h_ref, *, tm, chunk, tn):
    _norm_to_scratch(x_ref, nw_ref, h_ref, tm, chunk)
    h = h_ref[...]
    n_main = o_ref.shape[1]
    for c0 in range(0, w_ref.shape[1], tn):
        acc = jnp.dot(h, w_ref[:, c0:c0 + tn], preferred_element_type=F32)
        if c0 < n_main:
            o_ref[:, c0:c0 + tn] = acc.astype(BF16)
        else:
            u_ref[:, c0 - n_main:c0 - n_main + tn] = acc


def _inproj(x2, norm_w, w_in_b, *, tm, tn):
    m, d = x2.shape
    n = w_in_b.shape[1]
    n_main = n - POOL_WIDTH
    return pl.pallas_call(
        functools.partial(_inproj_kernel, tm=tm, chunk=128, tn=tn),
        grid=(m // tm,),
        in_specs=[
            pl.BlockSpec((tm, d), lambda i: (i, 0)),
            pl.BlockSpec((1, d), lambda i: (0, 0)),
            pl.BlockSpec((d, n), lambda i: (0, 0), pipeline_mode=pl.Buffered(1)),
        ],
        out_specs=[
            pl.BlockSpec((tm, n_main), lambda i: (i, 0)),
            pl.BlockSpec((tm, POOL_WIDTH), lambda i: (i, 0)),
        ],
        out_shape=[
            jax.ShapeDtypeStruct((m, n_main), BF16),
            jax.ShapeDtypeStruct((m, POOL_WIDTH), F32),
        ],
        scratch_shapes=[pltpu.VMEM((tm, d), BF16)],
        compiler_params=_compiler_params(("parallel",)),
        name="inproj",
    )(x2, norm_w.reshape(1, d), w_in_b)


def _rope(x, cos, sin):
    half = x.shape[-1] // 2
    x1, x2 = x[:, :half], x[:, half:]
    return jnp.concatenate([x1 * cos - x2 * sin, x2 * cos + x1 * sin], axis=-1)


def _retention_kernel(q_ref, k_ref, v_ref, g_ref, s0_ref, cos_ref, sin_ref, nw_ref,
                      o_ref, s_ref, *, bb, chunk, log_gammas):
    @pl.when(pl.program_id(1) == 0)
    def _():
        s_ref[...] = s0_ref[...]

    cos = cos_ref[...]
    sin = sin_ref[...]
    row = lax.broadcasted_iota(jnp.int32, (chunk, chunk), 0)
    col = lax.broadcasted_iota(jnp.int32, (chunk, chunk), 1)
    diff = (row - col).astype(F32)
    pos = lax.broadcasted_iota(jnp.int32, (chunk, 1), 0).astype(F32)

    for h, lg in enumerate(log_gammas):
        cs = slice(h * RET_DK, (h + 1) * RET_DK)
        decay = jnp.where(diff >= 0, jnp.exp(jnp.maximum(diff, 0.0) * lg), 0.0)
        q_dec = jnp.exp((pos + 1.0) * lg)
        k_dec = jnp.exp((chunk - 1.0 - pos) * lg)
        for bi in range(bb):
            qr = _rope(q_ref[bi, :, cs].astype(F32), cos, sin)
            kr = _rope(k_ref[bi, :, cs].astype(F32), cos, sin) * (RET_DK ** -0.5)
            vb = v_ref[bi, :, cs]
            s_old = s_ref[bi, h]

            qb = qr.astype(BF16)
            scores = lax.dot_general(qb, kr.astype(BF16), (((1,), (1,)), ((), ())),
                                     preferred_element_type=F32) * decay
            o_inner = jnp.dot(scores.astype(BF16), vb, preferred_element_type=F32)
            o_cross = jnp.dot(qb, s_old.astype(BF16), preferred_element_type=F32) * q_dec
            kd = (kr * k_dec).astype(BF16)
            s_ref[bi, h] = math.exp(chunk * lg) * s_old + lax.dot_general(
                kd, vb, (((0,), (0,)), ((), ())), preferred_element_type=F32)

            o = o_inner + o_cross
            o = o * lax.rsqrt(jnp.mean(o * o, axis=-1, keepdims=True) + EPS)
            o = o * nw_ref[:, cs]
            o_ref[bi, :, cs] = (_silu(g_ref[bi, :, cs].astype(F32)) * o).astype(BF16)


def _retention(qkvg3, s0, cos, sin, ret_norm_w, *, bb):
    b, l, _ = qkvg3.shape
    chunk = RET_CHUNK if l % RET_CHUNK == 0 else l
    log_gammas = tuple(math.log(1.0 - 2.0 ** (-5.0 - h)) for h in range(RET_HEADS))
    half = RET_DK // 2

    def col_block(idx):
        return pl.BlockSpec((bb, chunk, RET_WIDTH), lambda i, c: (i, c, idx))

    state_spec = pl.BlockSpec((bb, RET_HEADS, RET_DK, RET_DV), lambda i, c: (i, 0, 0, 0))
    return pl.pallas_call(
        functools.partial(_retention_kernel, bb=bb, chunk=chunk, log_gammas=log_gammas),
        grid=(b // bb, l // chunk),
        in_specs=[
            col_block(0), col_block(1), col_block(2), col_block(3),
            state_spec,
            pl.BlockSpec((chunk, half), lambda i, c: (c, 0)),
            pl.BlockSpec((chunk, half), lambda i, c: (c, 0)),
            pl.BlockSpec((1, RET_WIDTH), lambda i, c: (0, 0)),
        ],
        out_specs=[
            pl.BlockSpec((bb, chunk, RET_WIDTH), lambda i, c: (i, c, 0)),
            state_spec,
        ],
        out_shape=[
            jax.ShapeDtypeStruct((b, l, RET_WIDTH), BF16),
            jax.ShapeDtypeStruct((b, RET_HEADS, RET_DK, RET_DV), F32),
        ],
        compiler_params=_compiler_params(("parallel", "arbitrary")),
        name="retention",
    )(qkvg3, qkvg3, qkvg3, qkvg3, s0, cos, sin, ret_norm_w.reshape(1, RET_WIDTH))


def _pool_kernel(u_ref, buf_ref, wp_ref, ps_ref, o_ref, nb_ref, xp_ref, *, bb, tl, start):
    t = pl.program_id(1)

    @pl.when(t == 0)
    def _():
        xp_ref[:, 0:1, :] = jnp.zeros((bb, 1, POOL_WIDTH), F32)
        xp_ref[:, 1:POOL_HIST, :] = buf_ref[...]

    xp_ref[:, POOL_HIST:POOL_HIST + tl, :] = u_ref[...]
    pos = (lax.broadcasted_iota(jnp.int32, (bb, tl, POOL_GC), 1) + t * tl).astype(F32) + start

    for g, w in enumerate(POOL_WINDOWS):
        cs = slice(g * POOL_GC, (g + 1) * POOL_GC)
        u = xp_ref[:, POOL_HIST:POOL_HIST + tl, cs]
        acc = u
        for j in range(1, w):
            acc = acc + xp_ref[:, POOL_HIST - j:POOL_HIST - j + tl, cs]
        cnt = jnp.minimum(pos + 1.0, float(w))
        pooled = (acc / cnt - u).astype(BF16).reshape(bb * tl, POOL_GC)
        y = jnp.dot(pooled, wp_ref[g].astype(BF16), preferred_element_type=F32)
        y = y * ps_ref[:, cs]
        o_ref[:, :, cs] = y.reshape(bb, tl, POOL_GC).astype(BF16)

    nb_ref[...] = xp_ref[:, tl + 1:tl + POOL_HIST, :]
    if tl >= POOL_HIST:
        xp_ref[:, 0:POOL_HIST, :] = xp_ref[:, tl:tl + POOL_HIST, :]


def _pool(u3, buf, w_pool, pool_scale, start, *, bb, tl):
    b, l, _ = u3.shape
    assert l == tl or tl >= POOL_HIST
    return pl.pallas_call(
        functools.partial(_pool_kernel, bb=bb, tl=tl, start=float(start)),
        grid=(b // bb, l // tl),
        in_specs=[
            pl.BlockSpec((bb, tl, POOL_WIDTH), lambda i, t: (i, t, 0)),
            pl.BlockSpec((bb, POOL_BUF, POOL_WIDTH), lambda i, t: (i, 0, 0)),
            pl.BlockSpec(w_pool.shape, lambda i, t: (0, 0, 0)),
            pl.BlockSpec((1, POOL_WIDTH), lambda i, t: (0, 0)),
        ],
        out_specs=[
            pl.BlockSpec((bb, tl, POOL_WIDTH), lambda i, t: (i, t, 0)),
            pl.BlockSpec((bb, POOL_BUF, POOL_WIDTH), lambda i, t: (i, 0, 0)),
        ],
        out_shape=[
            jax.ShapeDtypeStruct((b, l, POOL_WIDTH), BF16),
            jax.ShapeDtypeStruct((b, POOL_BUF, POOL_WIDTH), F32),
        ],
        scratch_shapes=[pltpu.VMEM((bb, POOL_HIST + tl, POOL_WIDTH), F32)],
        compiler_params=_compiler_params(("parallel", "arbitrary")),
        name="pool",
    )(u3, buf, w_pool, pool_scale.reshape(1, POOL_WIDTH))


def _outproj_kernel(r_ref, p_ref, w_ref, x_ref, nw_ref, o_ref):
    mix = jnp.dot(r_ref[...], w_ref[0:RET_WIDTH, :], preferred_element_type=F32)
    mix = mix + jnp.dot(p_ref[...], w_ref[RET_WIDTH:, :], preferred_element_type=F32)
    o_ref[...] = x_ref[...] + _rms_rows(mix, nw_ref[...])


def _outproj(ret_out, pool_out, w_out_b, x2, norm_w, *, tm):
    m, d = x2.shape
    return pl.pallas_call(
        _outproj_kernel,
        grid=(m // tm,),
        in_specs=[
            pl.BlockSpec((tm, RET_WIDTH), lambda i: (i, 0)),
            pl.BlockSpec((tm, POOL_WIDTH), lambda i: (i, 0)),
            pl.BlockSpec(w_out_b.shape, lambda i: (0, 0)),
            pl.BlockSpec((tm, d), lambda i: (i, 0)),
            pl.BlockSpec((1, d), lambda i: (0, 0)),
        ],
        out_specs=pl.BlockSpec((tm, d), lambda i: (i, 0)),
        out_shape=jax.ShapeDtypeStruct((m, d), F32),
        compiler_params=_compiler_params(("parallel",)),
        name="outproj",
    )(ret_out, pool_out, w_out_b, x2, norm_w.reshape(1, d))


def _ffn_kernel(x_ref, nw1_ref, wg_ref, wu_ref, wd_ref, nw2_ref, o_ref, h_ref,
                *, tm, chunk, tf, sub):
    j = pl.program_id(1)

    @pl.when(j == 0)
    def _():
        _norm_to_scratch(x_ref, nw1_ref, h_ref, tm, chunk)
        o_ref[...] = jnp.zeros(o_ref.shape, F32)

    h = h_ref[...]
    part = None
    for s in range(tf // sub):
        cs = slice(s * sub, (s + 1) * sub)
        gate = jnp.dot(h, wg_ref[:, cs], preferred_element_type=F32)
        up = jnp.dot(h, wu_ref[:, cs], preferred_element_type=F32)
        act = (_silu(gate) * up).astype(BF16)
        p = jnp.dot(act, wd_ref[cs, :], preferred_element_type=F32)
        part = p if part is None else part + p
    o_ref[...] += part

    @pl.when(j == pl.num_programs(1) - 1)
    def _():
        o_ref[...] = x_ref[...] + _rms_rows(o_ref[...], nw2_ref[...])


def _ffn(x1, norm_pre, wg_b, wu_b, wd_b, norm_post, *, tm, tf):
    m, d = x1.shape
    dff = wg_b.shape[1]
    return pl.pallas_call(
        functools.partial(_ffn_kernel, tm=tm, chunk=128, tf=tf, sub=256),
        grid=(m // tm, dff // tf),
        in_specs=[
            pl.BlockSpec((tm, d), lambda i, j: (i, 0)),
            pl.BlockSpec((1, d), lambda i, j: (0, 0)),
            pl.BlockSpec((d, tf), lambda i, j: (0, j)),
            pl.BlockSpec((d, tf), lambda i, j: (0, j)),
            pl.BlockSpec((tf, d), lambda i, j: (j, 0)),
            pl.BlockSpec((1, d), lambda i, j: (0, 0)),
        ],
        out_specs=pl.BlockSpec((tm, d), lambda i, j: (i, 0)),
        out_shape=jax.ShapeDtypeStruct((m, d), F32),
        scratch_shapes=[pltpu.VMEM((tm, d), BF16)],
        compiler_params=_compiler_params(("parallel", "arbitrary")),
        name="ffn",
    )(x1, norm_pre.reshape(1, d), wg_b, wu_b, wd_b, norm_post.reshape(1, d))


def _rope_tables(length, start):
    pos = start + jnp.arange(length, dtype=F32)
    inv_freq = 1.0 / (ROPE_BASE ** (jnp.arange(0, RET_DK, 2, dtype=F32) / RET_DK))
    ang = pos[:, None] * inv_freq[None, :]
    return jnp.cos(ang), jnp.sin(ang)


def _layer(x, s0, pool_buf, start, weights, *, ret_bb, pool_bb, pool_tl):
    (norm_mix_pre, norm_mix_post, w_in_b, ret_norm_w, w_pool, pool_scale, w_out_b,
     norm_ffn_pre, norm_ffn_post, wg_b, wu_b, wd_b) = weights
    b, l, d = x.shape
    m = b * l
    x2 = x.reshape(m, d)
    qkvg, u = _inproj(x2, norm_mix_pre, w_in_b, tm=512, tn=512)
    cos, sin = _rope_tables(l, start)
    ret_out, s_new = _retention(qkvg.reshape(b, l, -1), s0, cos, sin, ret_norm_w, bb=ret_bb)
    pool_out, new_buf = _pool(u.reshape(b, l, -1), pool_buf, w_pool, pool_scale, start,
                              bb=pool_bb, tl=pool_tl)
    x1 = _outproj(ret_out.reshape(m, RET_WIDTH), pool_out.reshape(m, POOL_WIDTH),
                  w_out_b, x2, norm_mix_post, tm=512)
    y = _ffn(x1, norm_ffn_pre, wg_b, wu_b, wd_b, norm_ffn_post, tm=512, tf=512)
    return y.reshape(b, l, d), s_new, new_buf


def kernel(x_prompt, x_sample, state_ret, state_pool, norm_mix_pre, norm_mix_post, w_in,
           ret_norm_w, w_pool, pool_scale, w_out, norm_ffn_pre, norm_ffn_post,
           w_gate, w_up, w_down):
    weights = (norm_mix_pre, norm_mix_post, w_in.astype(BF16), ret_norm_w,
               w_pool, pool_scale, w_out.astype(BF16), norm_ffn_pre, norm_ffn_post,
               w_gate.astype(BF16), w_up.astype(BF16), w_down.astype(BF16))
    bp = x_prompt.shape[0]
    s0_prompt = jnp.zeros((bp,) + state_ret.shape[1:], F32)
    buf_prompt = jnp.zeros((bp,) + state_pool.shape[1:], x_prompt.dtype)
    yp, sp, new_bp = _layer(x_prompt, s0_prompt, buf_prompt, 0.0, weights,
                            ret_bb=1, pool_bb=1, pool_tl=256)
    ys, ss, new_bs = _layer(x_sample, state_ret, state_pool, float(PAST_LEN), weights,
                            ret_bb=4, pool_bb=16, pool_tl=x_sample.shape[1])
    return (yp, ys, sp.astype(x_prompt.dtype), new_bp.astype(x_prompt.dtype),
            ss.astype(state_ret.dtype), new_bs.astype(state_pool.dtype))
```

```python
import functools
import math

import jax
import jax.numpy as jnp
from jax import lax
from jax.experimental import pallas as pl
from jax.experimental.pallas import tpu as pltpu

F32 = jnp.float32
BF16 = jnp.bfloat16

RET_HEADS = 4
RET_DK = 256
RET_DV = 256
RET_WIDTH = RET_HEADS * RET_DV
RET_CHUNK = 128
ROPE_BASE = 10000.0
POOL_WINDOWS = (2, 4, 8, 16)
POOL_GC = 256
POOL_WIDTH = POOL_GC * len(POOL_WINDOWS)
POOL_BUF = max(POOL_WINDOWS) - 1
POOL_HIST = POOL_BUF + 1
PAST_LEN = 16384
EPS = 1e-6

VMEM_LIMIT_BYTES = 56 * 1024 * 1024


def _compiler_params(semantics):
    return pltpu.CompilerParams(dimension_semantics=semantics,
                                vmem_limit_bytes=VMEM_LIMIT_BYTES)


def _rms_rows(x, w):
    ms = jnp.mean(x * x, axis=-1, keepdims=True)
    return (x * lax.rsqrt(ms + EPS)) * w


def _silu(g):
    return g * (1.0 / (1.0 + jnp.exp(-g)))


def _norm_to_scratch(x_ref, w_ref, h_ref, rows, chunk):
    def body(r, carry):
        sl = pl.ds(pl.multiple_of(r * chunk, chunk), chunk)
        h_ref[sl, :] = _rms_rows(x_ref[sl, :], w_ref[...]).astype(BF16)
        return carry
    lax.fori_loop(0, rows // chunk, body, 0)


def _inproj_kernel(x_ref, nw_ref, w_ref, o_ref, u_ref, h_ref, *, tm, chunk, tn):
    _norm_to_scratch(x_ref, nw_ref, h_ref, tm, chunk)
    h = h_ref[...]
    n_main = o_ref.shape[1]
    for c0 in range(0, w_ref.shape[1], tn):
        acc = jnp.dot(h, w_ref[:, c0:c0 + tn], preferred_element_type=F32)
        if c0 < n_main:
            o_ref[:, c0:c0 + tn] = acc.astype(BF16)
        else:
            u_ref[:, c0 - n_main:c0 - n_main + tn] = acc


def _inproj(x2, norm_w, w_in_b, *, tm, tn):
    m, d = x2.shape
    n = w_in_b.shape[1]
    n_main = n - POOL_WIDTH
    return pl.pallas_call(
        functools.partial(_inproj_kernel, tm=tm, chunk=128, tn=tn),
        grid=(m // tm,),
        in_specs=[
            pl.BlockSpec((tm, d), lambda i: (i, 0)),
            pl.BlockSpec((1, d), lambda i: (0, 0)),
            pl.BlockSpec((d, n), lambda i: (0, 0), pipeline_mode=pl.Buffered(1)),
        ],
        out_specs=[
            pl.BlockSpec((tm, n_main), lambda i: (i, 0)),
            pl.BlockSpec((tm, POOL_WIDTH), lambda i: (i, 0)),
        ],
        out_shape=[
            jax.ShapeDtypeStruct((m, n_main), BF16),
            jax.ShapeDtypeStruct((m, POOL_WIDTH), F32),
        ],
        scratch_shapes=[pltpu.VMEM((tm, d), BF16)],
        compiler_params=_compiler_params(("parallel",)),
        name="inproj",
    )(x2, norm_w.reshape(1, d), w_in_b)


def _rope(x, cos, sin):
    half = x.shape[-1] // 2
    x1, x2 = x[:, :half], x[:, half:]
    return jnp.concatenate([x1 * cos - x2 * sin, x2 * cos + x1 * sin], axis=-1)


def _retention_kernel(q_ref, k_ref, v_ref, g_ref, s0_ref, cos_ref, sin_ref, cosk_ref, sink_ref,
                      decay_ref, nw_ref, o_ref, s_ref, *, bb, chunk, log_gammas):
    @pl.when(pl.program_id(1) == 0)
    def _():
        s_ref[...] = s0_ref[...]

    cos = cos_ref[...]
    sin = sin_ref[...]
    cos_k = cosk_ref[...]
    sin_k = sink_ref[...]
    pos = lax.broadcasted_iota(jnp.int32, (chunk, 1), 0).astype(F32)

    for h, lg in enumerate(log_gammas):
        cs = slice(h * RET_DK, (h + 1) * RET_DK)
        decay = decay_ref[h]
        q_dec = jnp.exp((pos + 1.0) * lg)
        k_dec = jnp.exp((chunk - 1.0 - pos) * lg)
        for bi in range(bb):
            qr = _rope(q_ref[bi, :, cs].astype(F32), cos, sin)
            kr = _rope(k_ref[bi, :, cs].astype(F32), cos_k, sin_k)
            vb = v_ref[bi, :, cs]
            s_old = s_ref[bi, h]

            qb = qr.astype(BF16)
            scores = lax.dot_general(qb, kr.astype(BF16), (((1,), (1,)), ((), ())),
                                     preferred_element_type=F32) * decay
            o_inner = jnp.dot(scores.astype(BF16), vb, preferred_element_type=F32)
            o_cross = jnp.dot(qb, s_old.astype(BF16), preferred_element_type=F32) * q_dec
            kd = (kr * k_dec).astype(BF16)
            s_ref[bi, h] = math.exp(chunk * lg) * s_old + lax.dot_general(
                kd, vb, (((0,), (0,)), ((), ())), preferred_element_type=F32)

            o = o_inner + o_cross
            o = o * lax.rsqrt(jnp.mean(o * o, axis=-1, keepdims=True) + EPS)
            o = o * nw_ref[:, cs]
            o_ref[bi, :, cs] = (_silu(g_ref[bi, :, cs].astype(F32)) * o).astype(BF16)


def _retention(qkvg3, s0, start, ret_norm_w, *, bb):
    b, l, _ = qkvg3.shape
    chunk = RET_CHUNK if l % RET_CHUNK == 0 else l
    log_gammas = tuple(math.log(1.0 - 2.0 ** (-5.0 - h)) for h in range(RET_HEADS))
    half = RET_DK // 2

    cos, sin = _rope_tables(l, start)
    k_scale = RET_DK ** -0.5
    idx = jnp.arange(chunk, dtype=F32)
    diff = idx[:, None] - idx[None, :]
    lg = jnp.asarray(log_gammas, F32)[:, None, None]
    decay = jnp.where(diff[None] >= 0, jnp.exp(jnp.maximum(diff[None], 0.0) * lg), 0.0)

    def col_block(idx):
        return pl.BlockSpec((bb, chunk, RET_WIDTH), lambda i, c: (i, c, idx))

    state_spec = pl.BlockSpec((bb, RET_HEADS, RET_DK, RET_DV), lambda i, c: (i, 0, 0, 0))
    table_spec = pl.BlockSpec((chunk, half), lambda i, c: (c, 0))
    return pl.pallas_call(
        functools.partial(_retention_kernel, bb=bb, chunk=chunk, log_gammas=log_gammas),
        grid=(b // bb, l // chunk),
        in_specs=[
            col_block(0), col_block(1), col_block(2), col_block(3),
            state_spec,
            table_spec, table_spec, table_spec, table_spec,
            pl.BlockSpec((RET_HEADS, chunk, chunk), lambda i, c: (0, 0, 0)),
            pl.BlockSpec((1, RET_WIDTH), lambda i, c: (0, 0)),
        ],
        out_specs=[
            pl.BlockSpec((bb, chunk, RET_WIDTH), lambda i, c: (i, c, 0)),
            state_spec,
        ],
        out_shape=[
            jax.ShapeDtypeStruct((b, l, RET_WIDTH), BF16),
            jax.ShapeDtypeStruct((b, RET_HEADS, RET_DK, RET_DV), F32),
        ],
        compiler_params=_compiler_params(("parallel", "arbitrary")),
        name="retention",
    )(qkvg3, qkvg3, qkvg3, qkvg3, s0, cos, sin, cos * k_scale, sin * k_scale, decay,
      ret_norm_w.reshape(1, RET_WIDTH))


def _pool_kernel(u_ref, buf_ref, wp_ref, ps_ref, o_ref, nb_ref, xp_ref, *, bb, tl, start):
    t = pl.program_id(1)

    @pl.when(t == 0)
    def _():
        xp_ref[:, 0:1, :] = jnp.zeros((bb, 1, POOL_WIDTH), F32)
        xp_ref[:, 1:POOL_HIST, :] = buf_ref[...]

    xp_ref[:, POOL_HIST:POOL_HIST + tl, :] = u_ref[...]
    pos = (lax.broadcasted_iota(jnp.int32, (bb, tl, POOL_GC), 1) + t * tl).astype(F32) + start
    rows = POOL_HIST + tl

    for g, w in enumerate(POOL_WINDOWS):
        cs = slice(g * POOL_GC, (g + 1) * POOL_GC)
        u = xp_ref[:, POOL_HIST:POOL_HIST + tl, cs]
        acc = xp_ref[:, :, cs].reshape(bb * rows, POOL_GC)
        shift = 1
        while shift < w:
            acc = acc + pltpu.roll(acc, shift, 0)
            shift *= 2
        acc = acc.reshape(bb, rows, POOL_GC)[:, POOL_HIST:, :]
        cnt = jnp.minimum(pos + 1.0, float(w))
        pooled = (acc / cnt - u).astype(BF16).reshape(bb * tl, POOL_GC)
        y = jnp.dot(pooled, wp_ref[g].astype(BF16), preferred_element_type=F32)
        y = y * ps_ref[:, cs]
        o_ref[:, :, cs] = y.reshape(bb, tl, POOL_GC).astype(BF16)

    nb_ref[...] = xp_ref[:, tl + 1:tl + POOL_HIST, :]
    if tl >= POOL_HIST:
        xp_ref[:, 0:POOL_HIST, :] = xp_ref[:, tl:tl + POOL_HIST, :]


def _pool(u3, buf, w_pool, pool_scale, start, *, bb, tl):
    b, l, _ = u3.shape
    assert l == tl or tl >= POOL_HIST
    return pl.pallas_call(
        functools.partial(_pool_kernel, bb=bb, tl=tl, start=float(start)),
        grid=(b // bb, l // tl),
        in_specs=[
            pl.BlockSpec((bb, tl, POOL_WIDTH), lambda i, t: (i, t, 0)),
            pl.BlockSpec((bb, POOL_BUF, POOL_WIDTH), lambda i, t: (i, 0, 0)),
            pl.BlockSpec(w_pool.shape, lambda i, t: (0, 0, 0)),
            pl.BlockSpec((1, POOL_WIDTH), lambda i, t: (0, 0)),
        ],
        out_specs=[
            pl.BlockSpec((bb, tl, POOL_WIDTH), lambda i, t: (i, t, 0)),
            pl.BlockSpec((bb, POOL_BUF, POOL_WIDTH), lambda i, t: (i, 0, 0)),
        ],
        out_shape=[
            jax.ShapeDtypeStruct((b, l, POOL_WIDTH), BF16),
            jax.ShapeDtypeStruct((b, POOL_BUF, POOL_WIDTH), F32),
        ],
        scratch_shapes=[pltpu.VMEM((bb, POOL_HIST + tl, POOL_WIDTH), F32)],
        compiler_params=_compiler_params(("parallel", "arbitrary")),
        name="pool",
    )(u3, buf, w_pool, pool_scale.reshape(1, POOL_WIDTH))


def _outproj_kernel(r_ref, p_ref, w_ref, x_ref, nw_ref, o_ref, *, tm, sub):
    for r0 in range(0, tm, sub):
        rs = slice(r0, r0 + sub)
        mix = jnp.dot(r_ref[rs, :], w_ref[0:RET_WIDTH, :], preferred_element_type=F32)
        mix = mix + jnp.dot(p_ref[rs, :], w_ref[RET_WIDTH:, :], preferred_element_type=F32)
        o_ref[rs, :] = x_ref[rs, :] + _rms_rows(mix, nw_ref[...])


def _outproj(ret_out, pool_out, w_out_b, x2, norm_w, *, tm, sub):
    m, d = x2.shape
    return pl.pallas_call(
        functools.partial(_outproj_kernel, tm=tm, sub=sub),
        grid=(m // tm,),
        in_specs=[
            pl.BlockSpec((tm, RET_WIDTH), lambda i: (i, 0)),
            pl.BlockSpec((tm, POOL_WIDTH), lambda i: (i, 0)),
            pl.BlockSpec(w_out_b.shape, lambda i: (0, 0), pipeline_mode=pl.Buffered(1)),
            pl.BlockSpec((tm, d), lambda i: (i, 0)),
            pl.BlockSpec((1, d), lambda i: (0, 0)),
        ],
        out_specs=pl.BlockSpec((tm, d), lambda i: (i, 0)),
        out_shape=jax.ShapeDtypeStruct((m, d), F32),
        compiler_params=_compiler_params(("parallel",)),
        name="outproj",
    )(ret_out, pool_out, w_out_b, x2, norm_w.reshape(1, d))


def _ffn_kernel(x_ref, nw1_ref, wg_ref, wu_ref, wd_ref, nw2_ref, o_ref, h_ref,
                *, tm, chunk, tf, sub):
    j = pl.program_id(1)

    @pl.when(j == 0)
    def _():
        _norm_to_scratch(x_ref, nw1_ref, h_ref, tm, chunk)
        o_ref[...] = jnp.zeros(o_ref.shape, F32)

    h = h_ref[...]
    part = None
    for s in range(tf // sub):
        cs = slice(s * sub, (s + 1) * sub)
        gate = jnp.dot(h, wg_ref[:, cs], preferred_element_type=F32)
        up = jnp.dot(h, wu_ref[:, cs], preferred_element_type=F32)
        act = (_silu(gate) * up).astype(BF16)
        p = jnp.dot(act, wd_ref[cs, :], preferred_element_type=F32)
        part = p if part is None else part + p
    o_ref[...] += part

    @pl.when(j == pl.num_programs(1) - 1)
    def _():
        o_ref[...] = x_ref[...] + _rms_rows(o_ref[...], nw2_ref[...])


def _ffn(x1, norm_pre, wg_b, wu_b, wd_b, norm_post, *, tm, tf):
    m, d = x1.shape
    dff = wg_b.shape[1]
    return pl.pallas_call(
        functools.partial(_ffn_kernel, tm=tm, chunk=128, tf=tf, sub=256),
        grid=(m // tm, dff // tf),
        in_specs=[
            pl.BlockSpec((tm, d), lambda i, j: (i, 0)),
            pl.BlockSpec((1, d), lambda i, j: (0, 0)),
            pl.BlockSpec((d, tf), lambda i, j: (0, j)),
            pl.BlockSpec((d, tf), lambda i, j: (0, j)),
            pl.BlockSpec((tf, d), lambda i, j: (j, 0)),
            pl.BlockSpec((1, d), lambda i, j: (0, 0)),
        ],
        out_specs=pl.BlockSpec((tm, d), lambda i, j: (i, 0)),
        out_shape=jax.ShapeDtypeStruct((m, d), F32),
        scratch_shapes=[pltpu.VMEM((tm, d), BF16)],
        compiler_params=_compiler_params(("parallel", "arbitrary")),
        name="ffn",
    )(x1, norm_pre.reshape(1, d), wg_b, wu_b, wd_b, norm_post.reshape(1, d))


def _rope_tables(length, start):
    pos = start + jnp.arange(length, dtype=F32)
    inv_freq = 1.0 / (ROPE_BASE ** (jnp.arange(0, RET_DK, 2, dtype=F32) / RET_DK))
    ang = pos[:, None] * inv_freq[None, :]
    return jnp.cos(ang), jnp.sin(ang)


def _layer(x, s0, pool_buf, start, weights, *, ret_bb, pool_bb, pool_tl):
    (norm_mix_pre, norm_mix_post, w_in_b, ret_norm_w, w_pool, pool_scale, w_out_b,
     norm_ffn_pre, norm_ffn_post, wg_b, wu_b, wd_b) = weights
    b, l, d = x.shape
    m = b * l
    x2 = x.reshape(m, d)
    qkvg, u = _inproj(x2, norm_mix_pre, w_in_b, tm=512, tn=512)
    ret_out, s_new = _retention(qkvg.reshape(b, l, -1), s0, start, ret_norm_w, bb=ret_bb)
    pool_out, new_buf = _pool(u.reshape(b, l, -1), pool_buf, w_pool, pool_scale, start,
                              bb=pool_bb, tl=pool_tl)
    x1 = _outproj(ret_out.reshape(m, RET_WIDTH), pool_out.reshape(m, POOL_WIDTH),
                  w_out_b, x2, norm_mix_post, tm=1024, sub=512)
    y = _ffn(x1, norm_ffn_pre, wg_b, wu_b, wd_b, norm_ffn_post, tm=512, tf=512)
    return y.reshape(b, l, d), s_new, new_buf


def kernel(x_prompt, x_sample, state_ret, state_pool, norm_mix_pre, norm_mix_post, w_in,
           ret_norm_w, w_pool, pool_scale, w_out, norm_ffn_pre, norm_ffn_post,
           w_gate, w_up, w_down):
    weights = (norm_mix_pre, norm_mix_post, w_in.astype(BF16), ret_norm_w,
               w_pool, pool_scale, w_out.astype(BF16), norm_ffn_pre, norm_ffn_post,
               w_gate.astype(BF16), w_up.astype(BF16), w_down.astype(BF16))
    bp = x_prompt.shape[0]
    s0_prompt = jnp.zeros((bp,) + state_ret.shape[1:], F32)
    buf_prompt = jnp.zeros((bp,) + state_pool.shape[1:], x_prompt.dtype)
    yp, sp, new_bp = _layer(x_prompt, s0_prompt, buf_prompt, 0.0, weights,
                            ret_bb=2, pool_bb=1, pool_tl=256)
    ys, ss, new_bs = _layer(x_sample, state_ret, state_pool, float(PAST_LEN), weights,
                            ret_bb=8, pool_bb=16, pool_tl=x_sample.shape[1])
    return (yp, ys, sp.astype(x_prompt.dtype), new_bp.astype(x_prompt.dtype),
            ss.astype(state_ret.dtype), new_bs.astype(state_pool.dtype))
```

```python
import functools
import math

import jax
import jax.numpy as jnp
from jax import lax
from jax.experimental import pallas as pl
from jax.experimental.pallas import tpu as pltpu

F32 = jnp.float32
BF16 = jnp.bfloat16

RET_HEADS = 4
RET_DK = 256
RET_DV = 256
RET_WIDTH = RET_HEADS * RET_DV
RET_CHUNK = 128
ROPE_BASE = 10000.0
POOL_WINDOWS = (2, 4, 8, 16)
POOL_GC = 256
POOL_WIDTH = POOL_GC * len(POOL_WINDOWS)
POOL_BUF = max(POOL_WINDOWS) - 1
POOL_HIST = POOL_BUF + 1
PAST_LEN = 16384
EPS = 1e-6

ROW_BLOCK = 512
MXU_TILE = 256
VMEM_LIMIT_BYTES = 56 * 1024 * 1024


def _compiler_params(semantics):
    return pltpu.CompilerParams(dimension_semantics=semantics,
                                vmem_limit_bytes=VMEM_LIMIT_BYTES)


def _rms_rows(x, w):
    ms = jnp.mean(x * x, axis=-1, keepdims=True)
    return (x * lax.rsqrt(ms + EPS)) * w


def _silu(g):
    return g * (1.0 / (1.0 + jnp.exp(-g)))


def _resident(shape, index_map):
    return pl.BlockSpec(shape, index_map, pipeline_mode=pl.Buffered(1))


def _two_group_specs(block, n_prompt_blocks):
    last = n_prompt_blocks - 1
    return (pl.BlockSpec(block, lambda i: (jnp.minimum(i, last), 0)),
            pl.BlockSpec(block, lambda i: (jnp.maximum(i - n_prompt_blocks, 0), 0)))


def _inproj_kernel(xp_ref, xs_ref, nw_ref, w_ref, o_ref, u_ref, h_ref,
                   *, tm, chunk, tn, n_prompt_blocks):
    is_prompt = pl.program_id(0) < n_prompt_blocks

    def body(r, carry):
        sl = pl.ds(pl.multiple_of(r * chunk, chunk), chunk)
        x = jnp.where(is_prompt, xp_ref[sl, :], xs_ref[sl, :])
        h_ref[sl, :] = _rms_rows(x, nw_ref[...]).astype(BF16)
        return carry
    lax.fori_loop(0, tm // chunk, body, 0)

    h = h_ref[...]
    n_main = o_ref.shape[1]
    for c0 in range(0, w_ref.shape[1], tn):
        acc = jnp.dot(h, w_ref[:, c0:c0 + tn], preferred_element_type=F32)
        if c0 < n_main:
            o_ref[:, c0:c0 + tn] = acc.astype(BF16)
        else:
            u_ref[:, c0 - n_main:c0 - n_main + tn] = acc


def _inproj(xp2, xs2, norm_w, w_in_b):
    tm = ROW_BLOCK
    d = xp2.shape[1]
    n = w_in_b.shape[1]
    n_main = n - POOL_WIDTH
    n_prompt_blocks = xp2.shape[0] // tm
    m = xp2.shape[0] + xs2.shape[0]
    xp_spec, xs_spec = _two_group_specs((tm, d), n_prompt_blocks)
    return pl.pallas_call(
        functools.partial(_inproj_kernel, tm=tm, chunk=128, tn=2 * MXU_TILE,
                          n_prompt_blocks=n_prompt_blocks),
        grid=(m // tm,),
        in_specs=[
            xp_spec, xs_spec,
            pl.BlockSpec((1, d), lambda i: (0, 0)),
            _resident((d, n), lambda i: (0, 0)),
        ],
        out_specs=[
            pl.BlockSpec((tm, n_main), lambda i: (i, 0)),
            pl.BlockSpec((tm, POOL_WIDTH), lambda i: (i, 0)),
        ],
        out_shape=[
            jax.ShapeDtypeStruct((m, n_main), BF16),
            jax.ShapeDtypeStruct((m, POOL_WIDTH), F32),
        ],
        scratch_shapes=[pltpu.VMEM((tm, d), BF16)],
        compiler_params=_compiler_params(("arbitrary",)),
        name="inproj",
    )(xp2, xs2, norm_w.reshape(1, d), w_in_b)


def _rope(x, cos, sin):
    half = x.shape[-1] // 2
    x1, x2 = x[:, :half], x[:, half:]
    return jnp.concatenate([x1 * cos - x2 * sin, x2 * cos + x1 * sin], axis=-1)


def _retention_kernel(q_ref, k_ref, v_ref, g_ref, s0_ref, cos_ref, sin_ref, cosk_ref, sink_ref,
                      decay_ref, nw_ref, o_ref, s_ref, *, n_seq, n_chunks, chunk, log_gammas):
    @pl.when(pl.program_id(1) == 0)
    def _():
        s_ref[...] = s0_ref[...]

    pos = lax.broadcasted_iota(jnp.int32, (chunk, 1), 0).astype(F32)

    for h, lg in enumerate(log_gammas):
        cs = slice(h * RET_DK, (h + 1) * RET_DK)
        decay = decay_ref[h]
        q_dec = jnp.exp((pos + 1.0) * lg)
        k_dec = jnp.exp((chunk - 1.0 - pos) * lg)
        for si in range(n_seq):
            for ci in range(n_chunks):
                rs = slice((si * n_chunks + ci) * chunk, (si * n_chunks + ci + 1) * chunk)
                ts = slice(ci * chunk, (ci + 1) * chunk)
                qr = _rope(q_ref[rs, cs].astype(F32), cos_ref[ts, :], sin_ref[ts, :])
                kr = _rope(k_ref[rs, cs].astype(F32), cosk_ref[ts, :], sink_ref[ts, :])
                vb = v_ref[rs, cs]
                s_old = s_ref[si, h]

                qb = qr.astype(BF16)
                scores = lax.dot_general(qb, kr.astype(BF16), (((1,), (1,)), ((), ())),
                                         preferred_element_type=F32) * decay
                o_inner = jnp.dot(scores.astype(BF16), vb, preferred_element_type=F32)
                o_cross = jnp.dot(qb, s_old.astype(BF16), preferred_element_type=F32) * q_dec
                kd = (kr * k_dec).astype(BF16)
                s_ref[si, h] = math.exp(chunk * lg) * s_old + lax.dot_general(
                    kd, vb, (((0,), (0,)), ((), ())), preferred_element_type=F32)

                o = o_inner + o_cross
                o = o * lax.rsqrt(jnp.mean(o * o, axis=-1, keepdims=True) + EPS)
                o = o * nw_ref[:, cs]
                o_ref[rs, cs] = (_silu(g_ref[rs, cs].astype(F32)) * o).astype(BF16)


def _retention(qkvg, row0, b, l, s0, start, ret_norm_w, *, n_seq, n_chunks):
    chunk = RET_CHUNK if l % RET_CHUNK == 0 else l
    log_gammas = tuple(math.log(1.0 - 2.0 ** (-5.0 - h)) for h in range(RET_HEADS))
    half = RET_DK // 2
    steps_per_seq = l // (chunk * n_chunks)
    block_rows = n_seq * n_chunks * chunk
    assert row0 % block_rows == 0 and (n_seq == 1 or steps_per_seq == 1)
    block0 = row0 // block_rows

    cos, sin = _rope_tables(l, start)
    k_scale = RET_DK ** -0.5
    idx = jnp.arange(chunk, dtype=F32)
    diff = idx[:, None] - idx[None, :]
    lg = jnp.asarray(log_gammas, F32)[:, None, None]
    decay = jnp.where(diff[None] >= 0, jnp.exp(jnp.maximum(diff[None], 0.0) * lg), 0.0)

    def col_block(idx):
        return pl.BlockSpec((block_rows, RET_WIDTH),
                            lambda i, c: (block0 + i * steps_per_seq + c, idx))

    state_spec = pl.BlockSpec((n_seq, RET_HEADS, RET_DK, RET_DV), lambda i, c: (i, 0, 0, 0))
    table_spec = pl.BlockSpec((n_chunks * chunk, half), lambda i, c: (c, 0))
    return pl.pallas_call(
        functools.partial(_retention_kernel, n_seq=n_seq, n_chunks=n_chunks, chunk=chunk,
                          log_gammas=log_gammas),
        grid=(b // n_seq, steps_per_seq),
        in_specs=[
            col_block(0), col_block(1), col_block(2), col_block(3),
            state_spec,
            table_spec, table_spec, table_spec, table_spec,
            pl.BlockSpec((RET_HEADS, chunk, chunk), lambda i, c: (0, 0, 0)),
            pl.BlockSpec((1, RET_WIDTH), lambda i, c: (0, 0)),
        ],
        out_specs=[
            pl.BlockSpec((block_rows, RET_WIDTH), lambda i, c: (i * steps_per_seq + c, 0)),
            state_spec,
        ],
        out_shape=[
            jax.ShapeDtypeStruct((b * l, RET_WIDTH), BF16),
            jax.ShapeDtypeStruct((b, RET_HEADS, RET_DK, RET_DV), F32),
        ],
        compiler_params=_compiler_params(("parallel", "arbitrary")),
        name="retention",
    )(qkvg, qkvg, qkvg, qkvg, s0, cos, sin, cos * k_scale, sin * k_scale, decay,
      ret_norm_w.reshape(1, RET_WIDTH))


def _pool_kernel(u_ref, buf_ref, wp_ref, ps_ref, o_ref, nb_ref, xp_ref, *, bb, tl, start):
    t = pl.program_id(1)

    @pl.when(t == 0)
    def _():
        xp_ref[:, 0:1, :] = jnp.zeros((bb, 1, POOL_WIDTH), F32)
        xp_ref[:, 1:POOL_HIST, :] = buf_ref[...]

    xp_ref[:, POOL_HIST:POOL_HIST + tl, :] = u_ref[...].reshape(bb, tl, POOL_WIDTH)
    pos = (lax.broadcasted_iota(jnp.int32, (bb, tl, POOL_GC), 1) + t * tl).astype(F32) + start
    rows = POOL_HIST + tl

    for g, w in enumerate(POOL_WINDOWS):
        cs = slice(g * POOL_GC, (g + 1) * POOL_GC)
        u = xp_ref[:, POOL_HIST:POOL_HIST + tl, cs]
        acc = xp_ref[:, :, cs].reshape(bb * rows, POOL_GC)
        shift = 1
        while shift < w:
            acc = acc + pltpu.roll(acc, shift, 0)
            shift *= 2
        acc = acc.reshape(bb, rows, POOL_GC)[:, POOL_HIST:, :]
        cnt = jnp.minimum(pos + 1.0, float(w))
        pooled = (acc / cnt - u).astype(BF16).reshape(bb * tl, POOL_GC)
        y = jnp.dot(pooled, wp_ref[g].astype(BF16), preferred_element_type=F32)
        o_ref[:, cs] = (y * ps_ref[:, cs]).astype(BF16)

    nb_ref[...] = xp_ref[:, tl + 1:tl + POOL_HIST, :]
    if tl >= POOL_HIST:
        xp_ref[:, 0:POOL_HIST, :] = xp_ref[:, tl:tl + POOL_HIST, :]


def _pool(u_all, row0, b, l, buf, w_pool, pool_scale, start, *, bb, tl):
    assert l == tl or (bb == 1 and tl >= POOL_HIST)
    steps_per_seq = l // tl
    block_rows = bb * tl
    assert row0 % block_rows == 0
    block0 = row0 // block_rows
    return pl.pallas_call(
        functools.partial(_pool_kernel, bb=bb, tl=tl, start=float(start)),
        grid=(b // bb, steps_per_seq),
        in_specs=[
            pl.BlockSpec((block_rows, POOL_WIDTH),
                         lambda i, t: (block0 + i * steps_per_seq + t, 0)),
            pl.BlockSpec((bb, POOL_BUF, POOL_WIDTH), lambda i, t: (i, 0, 0)),
            pl.BlockSpec(w_pool.shape, lambda i, t: (0, 0, 0)),
            pl.BlockSpec((1, POOL_WIDTH), lambda i, t: (0, 0)),
        ],
        out_specs=[
            pl.BlockSpec((block_rows, POOL_WIDTH), lambda i, t: (i * steps_per_seq + t, 0)),
            pl.BlockSpec((bb, POOL_BUF, POOL_WIDTH), lambda i, t: (i, 0, 0)),
        ],
        out_shape=[
            jax.ShapeDtypeStruct((b * l, POOL_WIDTH), BF16),
            jax.ShapeDtypeStruct((b, POOL_BUF, POOL_WIDTH), F32),
        ],
        scratch_shapes=[pltpu.VMEM((bb, POOL_HIST + tl, POOL_WIDTH), F32)],
        compiler_params=_compiler_params(("parallel", "arbitrary")),
        name="pool",
    )(u_all, buf, w_pool, pool_scale.reshape(1, POOL_WIDTH))


def _outproj_kernel(rp_ref, rs_ref, pp_ref, ps_ref, xp_ref, xs_ref, w_ref, nw1_ref, nw2_ref,
                    x1_ref, h2_ref, *, tm, sub, n_prompt_blocks):
    is_prompt = pl.program_id(0) < n_prompt_blocks
    for r0 in range(0, tm, sub):
        rs = slice(r0, r0 + sub)
        ret = jnp.where(is_prompt, rp_ref[rs, :], rs_ref[rs, :])
        pool = jnp.where(is_prompt, pp_ref[rs, :], ps_ref[rs, :])
        mix = jnp.dot(ret, w_ref[0:RET_WIDTH, :], preferred_element_type=F32)
        mix = mix + jnp.dot(pool, w_ref[RET_WIDTH:, :], preferred_element_type=F32)
        x1 = jnp.where(is_prompt, xp_ref[rs, :], xs_ref[rs, :]) + _rms_rows(mix, nw1_ref[...])
        x1_ref[rs, :] = x1
        h2_ref[rs, :] = _rms_rows(x1, nw2_ref[...]).astype(BF16)


def _outproj(ret_p, ret_s, pool_p, pool_s, xp2, xs2, w_out_b, norm_post, norm_ffn_pre):
    tm = ROW_BLOCK
    d = xp2.shape[1]
    n_prompt_blocks = xp2.shape[0] // tm
    m = xp2.shape[0] + xs2.shape[0]
    rp_spec, rs_spec = _two_group_specs((tm, RET_WIDTH), n_prompt_blocks)
    pp_spec, ps_spec = _two_group_specs((tm, POOL_WIDTH), n_prompt_blocks)
    xp_spec, xs_spec = _two_group_specs((tm, d), n_prompt_blocks)
    return pl.pallas_call(
        functools.partial(_outproj_kernel, tm=tm, sub=MXU_TILE, n_prompt_blocks=n_prompt_blocks),
        grid=(m // tm,),
        in_specs=[
            rp_spec, rs_spec, pp_spec, ps_spec, xp_spec, xs_spec,
            _resident(w_out_b.shape, lambda i: (0, 0)),
            pl.BlockSpec((1, d), lambda i: (0, 0)),
            pl.BlockSpec((1, d), lambda i: (0, 0)),
        ],
        out_specs=[
            pl.BlockSpec((tm, d), lambda i: (i, 0)),
            pl.BlockSpec((tm, d), lambda i: (i, 0)),
        ],
        out_shape=[
            jax.ShapeDtypeStruct((m, d), F32),
            jax.ShapeDtypeStruct((m, d), BF16),
        ],
        compiler_params=_compiler_params(("arbitrary",)),
        name="outproj",
    )(ret_p, ret_s, pool_p, pool_s, xp2, xs2, w_out_b,
      norm_post.reshape(1, d), norm_ffn_pre.reshape(1, d))


def _ffn_up_kernel(h_ref, wg_ref, wu_ref, a_ref, *, sub):
    h = h_ref[...]
    for c0 in range(0, a_ref.shape[1], sub):
        cs = slice(c0, c0 + sub)
        gate = jnp.dot(h, wg_ref[:, cs], preferred_element_type=F32)
        up = jnp.dot(h, wu_ref[:, cs], preferred_element_type=F32)
        a_ref[:, cs] = (_silu(gate) * up).astype(BF16)


def _ffn_up(h2, wg_b, wu_b, *, n_col_blocks):
    tm = ROW_BLOCK
    m, d = h2.shape
    dff = wg_b.shape[1]
    tf = dff // n_col_blocks
    assert tf % MXU_TILE == 0
    return pl.pallas_call(
        functools.partial(_ffn_up_kernel, sub=MXU_TILE),
        grid=(n_col_blocks, m // tm),
        in_specs=[
            pl.BlockSpec((tm, d), lambda j, i: (i, 0)),
            _resident((d, tf), lambda j, i: (0, j)),
            _resident((d, tf), lambda j, i: (0, j)),
        ],
        out_specs=pl.BlockSpec((tm, tf), lambda j, i: (i, j)),
        out_shape=jax.ShapeDtypeStruct((m, dff), BF16),
        compiler_params=_compiler_params(("arbitrary", "arbitrary")),
        name="ffn_up",
    )(h2, wg_b, wu_b)


def _ffn_down_kernel(a_ref, wd_ref, x1_ref, nw_ref, o_ref, *, tm, sub):
    for r0 in range(0, tm, sub):
        rs = slice(r0, r0 + sub)
        ff = jnp.dot(a_ref[rs, :], wd_ref[...], preferred_element_type=F32)
        o_ref[rs, :] = x1_ref[rs, :] + _rms_rows(ff, nw_ref[...])


def _ffn_down(act, x1, row0, rows, wd_b, norm_post):
    tm = ROW_BLOCK
    dff, d = wd_b.shape
    assert row0 % tm == 0 and rows % tm == 0
    block0 = row0 // tm
    return pl.pallas_call(
        functools.partial(_ffn_down_kernel, tm=tm, sub=MXU_TILE),
        grid=(rows // tm,),
        in_specs=[
            pl.BlockSpec((tm, dff), lambda i: (block0 + i, 0)),
            _resident((dff, d), lambda i: (0, 0)),
            pl.BlockSpec((tm, d), lambda i: (block0 + i, 0)),
            pl.BlockSpec((1, d), lambda i: (0, 0)),
        ],
        out_specs=pl.BlockSpec((tm, d), lambda i: (i, 0)),
        out_shape=jax.ShapeDtypeStruct((rows, d), F32),
        compiler_params=_compiler_params(("arbitrary",)),
        name="ffn_down",
    )(act, wd_b, x1, norm_post.reshape(1, d))


def _rope_tables(length, start):
    pos = start + jnp.arange(length, dtype=F32)
    inv_freq = 1.0 / (ROPE_BASE ** (jnp.arange(0, RET_DK, 2, dtype=F32) / RET_DK))
    ang = pos[:, None] * inv_freq[None, :]
    return jnp.cos(ang), jnp.sin(ang)


def kernel(x_prompt, x_sample, state_ret, state_pool, norm_mix_pre, norm_mix_post, w_in,
           ret_norm_w, w_pool, pool_scale, w_out, norm_ffn_pre, norm_ffn_post,
           w_gate, w_up, w_down):
    bp, lp, d = x_prompt.shape
    bs, ls, _ = x_sample.shape
    mp, ms = bp * lp, bs * ls
    xp2 = x_prompt.reshape(mp, d)
    xs2 = x_sample.reshape(ms, d)

    qkvg, u = _inproj(xp2, xs2, norm_mix_pre, w_in.astype(BF16))

    s0_prompt = jnp.zeros((bp,) + state_ret.shape[1:], F32)
    buf_prompt = jnp.zeros((bp,) + state_pool.shape[1:], F32)
    ret_p, s_p = _retention(qkvg, 0, bp, lp, s0_prompt, 0.0, ret_norm_w, n_seq=1, n_chunks=2)
    pool_p, buf_p = _pool(u, 0, bp, lp, buf_prompt, w_pool, pool_scale, 0.0, bb=1, tl=512)
    ret_s, s_s = _retention(qkvg, mp, bs, ls, state_ret, float(PAST_LEN), ret_norm_w,
                            n_seq=8, n_chunks=1)
    pool_s, buf_s = _pool(u, mp, bs, ls, state_pool, w_pool, pool_scale, float(PAST_LEN),
                          bb=16, tl=ls)

    x1, h2 = _outproj(ret_p, ret_s, pool_p, pool_s, xp2, xs2, w_out.astype(BF16),
                      norm_mix_post, norm_ffn_pre)
    act = _ffn_up(h2, w_gate.astype(BF16), w_up.astype(BF16), n_col_blocks=2)
    wd_b = w_down.astype(BF16)
    yp = _ffn_down(act, x1, 0, mp, wd_b, norm_ffn_post)
    ys = _ffn_down(act, x1, mp, ms, wd_b, norm_ffn_post)

    return (yp.reshape(bp, lp, d), ys.reshape(bs, ls, d),
            s_p.astype(x_prompt.dtype), buf_p.astype(x_prompt.dtype),
            s_s.astype(state_ret.dtype), buf_s.astype(state_pool.dtype))
```

```python
import functools
import math

import jax
import jax.numpy as jnp
from jax import lax
from jax.experimental import pallas as pl
from jax.experimental.pallas import tpu as pltpu

F32 = jnp.float32
BF16 = jnp.bfloat16

RET_HEADS = 4
RET_DK = 256
RET_DV = 256
RET_WIDTH = RET_HEADS * RET_DV
RET_CHUNK = 128
ROPE_BASE = 10000.0
POOL_WINDOWS = (2, 4, 8, 16)
POOL_GC = 256
POOL_WIDTH = POOL_GC * len(POOL_WINDOWS)
POOL_BUF = max(POOL_WINDOWS) - 1
POOL_HIST = POOL_BUF + 1
PAST_LEN = 16384
EPS = 1e-6

ROW_BLOCK = 512
INPROJ_ROW_BLOCK = 256
ROW_SUB_BLOCK = 128
MXU_TILE = 256
VMEM_LIMIT_BYTES = 56 * 1024 * 1024


def _compiler_params(semantics):
    return pltpu.CompilerParams(dimension_semantics=semantics,
                                vmem_limit_bytes=VMEM_LIMIT_BYTES)


def _rms_rows(x, w):
    ms = jnp.mean(x * x, axis=-1, keepdims=True)
    return (x * lax.rsqrt(ms + EPS)) * w


def _silu(g):
    return g * (1.0 / (1.0 + jnp.exp(-g)))


def _resident(shape, index_map):
    return pl.BlockSpec(shape, index_map, pipeline_mode=pl.Buffered(1))


def _two_group_specs(block, n_prompt_blocks):
    last = n_prompt_blocks - 1
    return (pl.BlockSpec(block, lambda i: (jnp.minimum(i, last), 0)),
            pl.BlockSpec(block, lambda i: (jnp.maximum(i - n_prompt_blocks, 0), 0)))


def _inproj_kernel(xp_ref, xs_ref, nw_ref, w_ref, *rest, tm, chunk, tn, n_prompt_blocks, n_cast):
    cast_src = rest[:n_cast]
    o_ref, u_ref = rest[n_cast:n_cast + 2]
    cast_dst = rest[n_cast + 2:2 * n_cast + 2]
    h_ref = rest[-1]
    is_prompt = pl.program_id(0) < n_prompt_blocks

    def body(r, carry):
        sl = pl.ds(pl.multiple_of(r * chunk, chunk), chunk)
        x = jnp.where(is_prompt, xp_ref[sl, :], xs_ref[sl, :])
        h_ref[sl, :] = _rms_rows(x, nw_ref[...]).astype(BF16)
        return carry
    lax.fori_loop(0, tm // chunk, body, 0)

    h = h_ref[...]
    n_main = o_ref.shape[1]
    for c0 in range(0, w_ref.shape[1], tn):
        acc = jnp.dot(h, w_ref[:, c0:c0 + tn], preferred_element_type=F32)
        if c0 < n_main:
            o_ref[:, c0:c0 + tn] = acc.astype(BF16)
        else:
            u_ref[:, c0 - n_main:c0 - n_main + tn] = acc

    for src, dst in zip(cast_src, cast_dst):
        dst[...] = src[...].astype(BF16)


def _inproj(xp2, xs2, norm_w, w_in_b, later_weights):
    tm = INPROJ_ROW_BLOCK
    d = xp2.shape[1]
    n = w_in_b.shape[1]
    n_main = n - POOL_WIDTH
    n_prompt_blocks = xp2.shape[0] // tm
    m = xp2.shape[0] + xs2.shape[0]
    xp_spec, xs_spec = _two_group_specs((tm, d), n_prompt_blocks)
    n_chunks = n_prompt_blocks
    cast_specs = [pl.BlockSpec((w.shape[0] // n_chunks, w.shape[1]),
                               lambda i: (jnp.minimum(i, n_chunks - 1), 0))
                  for w in later_weights]
    for w in later_weights:
        assert w.shape[0] % (16 * n_chunks) == 0
    outs = pl.pallas_call(
        functools.partial(_inproj_kernel, tm=tm, chunk=128, tn=2 * MXU_TILE,
                          n_prompt_blocks=n_prompt_blocks, n_cast=len(later_weights)),
        grid=(m // tm,),
        in_specs=[
            xp_spec, xs_spec,
            pl.BlockSpec((1, d), lambda i: (0, 0)),
            _resident((d, n), lambda i: (0, 0)),
            *cast_specs,
        ],
        out_specs=[
            pl.BlockSpec((tm, n_main), lambda i: (i, 0)),
            pl.BlockSpec((tm, POOL_WIDTH), lambda i: (i, 0)),
            *cast_specs,
        ],
        out_shape=[
            jax.ShapeDtypeStruct((m, n_main), BF16),
            jax.ShapeDtypeStruct((m, POOL_WIDTH), F32),
            *[jax.ShapeDtypeStruct(w.shape, BF16) for w in later_weights],
        ],
        scratch_shapes=[pltpu.VMEM((tm, d), BF16)],
        compiler_params=_compiler_params(("arbitrary",)),
        name="inproj",
    )(xp2, xs2, norm_w.reshape(1, d), w_in_b, *later_weights)
    return outs[0], outs[1], outs[2:]


def _rope(x, cos, sin):
    half = x.shape[-1] // 2
    x1, x2 = x[:, :half], x[:, half:]
    return jnp.concatenate([x1 * cos - x2 * sin, x2 * cos + x1 * sin], axis=-1)


def _retention_kernel(q_ref, k_ref, v_ref, g_ref, s0_ref, cos_ref, sin_ref, cosk_ref, sink_ref,
                      decay_ref, nw_ref, o_ref, s_ref, *, n_seq, n_chunks, chunk, log_gammas):
    @pl.when(pl.program_id(1) == 0)
    def _():
        s_ref[...] = s0_ref[...]

    pos = lax.broadcasted_iota(jnp.int32, (chunk, 1), 0).astype(F32)

    for h, lg in enumerate(log_gammas):
        cs = slice(h * RET_DK, (h + 1) * RET_DK)
        decay = decay_ref[h]
        q_dec = jnp.exp((pos + 1.0) * lg)
        k_dec = jnp.exp((chunk - 1.0 - pos) * lg)
        for si in range(n_seq):
            for ci in range(n_chunks):
                rs = slice((si * n_chunks + ci) * chunk, (si * n_chunks + ci + 1) * chunk)
                ts = slice(ci * chunk, (ci + 1) * chunk)
                qr = _rope(q_ref[rs, cs].astype(F32), cos_ref[ts, :], sin_ref[ts, :])
                kr = _rope(k_ref[rs, cs].astype(F32), cosk_ref[ts, :], sink_ref[ts, :])
                vb = v_ref[rs, cs]
                s_old = s_ref[si, h]

                qb = qr.astype(BF16)
                scores = lax.dot_general(qb, kr.astype(BF16), (((1,), (1,)), ((), ())),
                                         preferred_element_type=F32) * decay
                o_inner = jnp.dot(scores.astype(BF16), vb, preferred_element_type=F32)
                o_cross = jnp.dot(qb, s_old.astype(BF16), preferred_element_type=F32) * q_dec
                kd = (kr * k_dec).astype(BF16)
                s_ref[si, h] = math.exp(chunk * lg) * s_old + lax.dot_general(
                    kd, vb, (((0,), (0,)), ((), ())), preferred_element_type=F32)

                o = o_inner + o_cross
                o = o * lax.rsqrt(jnp.mean(o * o, axis=-1, keepdims=True) + EPS)
                o = o * nw_ref[:, cs]
                o_ref[rs, cs] = (_silu(g_ref[rs, cs].astype(F32)) * o).astype(BF16)


def _retention(qkvg, row0, b, l, s0, start, ret_norm_w, *, n_seq, n_chunks):
    chunk = RET_CHUNK if l % RET_CHUNK == 0 else l
    log_gammas = tuple(math.log(1.0 - 2.0 ** (-5.0 - h)) for h in range(RET_HEADS))
    half = RET_DK // 2
    steps_per_seq = l // (chunk * n_chunks)
    block_rows = n_seq * n_chunks * chunk
    assert row0 % block_rows == 0 and (n_seq == 1 or steps_per_seq == 1)
    block0 = row0 // block_rows

    cos, sin = _rope_tables(l, start)
    k_scale = RET_DK ** -0.5
    idx = jnp.arange(chunk, dtype=F32)
    diff = idx[:, None] - idx[None, :]
    lg = jnp.asarray(log_gammas, F32)[:, None, None]
    decay = jnp.where(diff[None] >= 0, jnp.exp(jnp.maximum(diff[None], 0.0) * lg), 0.0)

    def col_block(idx):
        return pl.BlockSpec((block_rows, RET_WIDTH),
                            lambda i, c: (block0 + i * steps_per_seq + c, idx))

    state_spec = pl.BlockSpec((n_seq, RET_HEADS, RET_DK, RET_DV), lambda i, c: (i, 0, 0, 0))
    table_spec = pl.BlockSpec((n_chunks * chunk, half), lambda i, c: (c, 0))
    return pl.pallas_call(
        functools.partial(_retention_kernel, n_seq=n_seq, n_chunks=n_chunks, chunk=chunk,
                          log_gammas=log_gammas),
        grid=(b // n_seq, steps_per_seq),
        in_specs=[
            col_block(0), col_block(1), col_block(2), col_block(3),
            state_spec,
            table_spec, table_spec, table_spec, table_spec,
            pl.BlockSpec((RET_HEADS, chunk, chunk), lambda i, c: (0, 0, 0)),
            pl.BlockSpec((1, RET_WIDTH), lambda i, c: (0, 0)),
        ],
        out_specs=[
            pl.BlockSpec((block_rows, RET_WIDTH), lambda i, c: (i * steps_per_seq + c, 0)),
            state_spec,
        ],
        out_shape=[
            jax.ShapeDtypeStruct((b * l, RET_WIDTH), BF16),
            jax.ShapeDtypeStruct((b, RET_HEADS, RET_DK, RET_DV), F32),
        ],
        compiler_params=_compiler_params(("parallel", "arbitrary")),
        name="retention",
    )(qkvg, qkvg, qkvg, qkvg, s0, cos, sin, cos * k_scale, sin * k_scale, decay,
      ret_norm_w.reshape(1, RET_WIDTH))


def _pool_kernel(u_ref, buf_ref, wp_ref, ps_ref, o_ref, nb_ref, xp_ref, *, bb, tl, start):
    t = pl.program_id(1)

    @pl.when(t == 0)
    def _():
        xp_ref[:, 0:1, :] = jnp.zeros((bb, 1, POOL_WIDTH), F32)
        xp_ref[:, 1:POOL_HIST, :] = buf_ref[...]

    xp_ref[:, POOL_HIST:POOL_HIST + tl, :] = u_ref[...].reshape(bb, tl, POOL_WIDTH)
    pos = (lax.broadcasted_iota(jnp.int32, (bb, tl, POOL_GC), 1) + t * tl).astype(F32) + start
    rows = POOL_HIST + tl

    for g, w in enumerate(POOL_WINDOWS):
        cs = slice(g * POOL_GC, (g + 1) * POOL_GC)
        u = xp_ref[:, POOL_HIST:POOL_HIST + tl, cs]
        acc = xp_ref[:, :, cs].reshape(bb * rows, POOL_GC)
        shift = 1
        while shift < w:
            acc = acc + pltpu.roll(acc, shift, 0)
            shift *= 2
        acc = acc.reshape(bb, rows, POOL_GC)[:, POOL_HIST:, :]
        cnt = jnp.minimum(pos + 1.0, float(w))
        pooled = (acc / cnt - u).astype(BF16).reshape(bb * tl, POOL_GC)
        y = jnp.dot(pooled, wp_ref[g].astype(BF16), preferred_element_type=F32)
        o_ref[:, cs] = (y * ps_ref[:, cs]).astype(BF16)

    nb_ref[...] = xp_ref[:, tl + 1:tl + POOL_HIST, :]
    if tl >= POOL_HIST:
        xp_ref[:, 0:POOL_HIST, :] = xp_ref[:, tl:tl + POOL_HIST, :]


def _pool(u_all, row0, b, l, buf, w_pool, pool_scale, start, *, bb, tl):
    assert l == tl or (bb == 1 and tl >= POOL_HIST)
    steps_per_seq = l // tl
    block_rows = bb * tl
    assert row0 % block_rows == 0
    block0 = row0 // block_rows
    return pl.pallas_call(
        functools.partial(_pool_kernel, bb=bb, tl=tl, start=float(start)),
        grid=(b // bb, steps_per_seq),
        in_specs=[
            pl.BlockSpec((block_rows, POOL_WIDTH),
                         lambda i, t: (block0 + i * steps_per_seq + t, 0)),
            pl.BlockSpec((bb, POOL_BUF, POOL_WIDTH), lambda i, t: (i, 0, 0)),
            pl.BlockSpec(w_pool.shape, lambda i, t: (0, 0, 0)),
            pl.BlockSpec((1, POOL_WIDTH), lambda i, t: (0, 0)),
        ],
        out_specs=[
            pl.BlockSpec((block_rows, POOL_WIDTH), lambda i, t: (i * steps_per_seq + t, 0)),
            pl.BlockSpec((bb, POOL_BUF, POOL_WIDTH), lambda i, t: (i, 0, 0)),
        ],
        out_shape=[
            jax.ShapeDtypeStruct((b * l, POOL_WIDTH), BF16),
            jax.ShapeDtypeStruct((b, POOL_BUF, POOL_WIDTH), F32),
        ],
        scratch_shapes=[pltpu.VMEM((bb, POOL_HIST + tl, POOL_WIDTH), F32)],
        compiler_params=_compiler_params(("parallel", "arbitrary")),
        name="pool",
    )(u_all, buf, w_pool, pool_scale.reshape(1, POOL_WIDTH))


def _outproj_kernel(rp_ref, rs_ref, pp_ref, ps_ref, xp_ref, xs_ref, w_ref, nw1_ref, nw2_ref,
                    x1_ref, h2_ref, *, tm, sub, n_prompt_blocks):
    is_prompt = pl.program_id(0) < n_prompt_blocks
    for r0 in range(0, tm, sub):
        rs = slice(r0, r0 + sub)
        ret = jnp.where(is_prompt, rp_ref[rs, :], rs_ref[rs, :])
        pool = jnp.where(is_prompt, pp_ref[rs, :], ps_ref[rs, :])
        mix = jnp.dot(ret, w_ref[0:RET_WIDTH, :], preferred_element_type=F32)
        mix = mix + jnp.dot(pool, w_ref[RET_WIDTH:, :], preferred_element_type=F32)
        x1 = jnp.where(is_prompt, xp_ref[rs, :], xs_ref[rs, :]) + _rms_rows(mix, nw1_ref[...])
        x1_ref[rs, :] = x1
        h2_ref[rs, :] = _rms_rows(x1, nw2_ref[...]).astype(BF16)


def _outproj(ret_p, ret_s, pool_p, pool_s, xp2, xs2, w_out_b, norm_post, norm_ffn_pre):
    tm = ROW_BLOCK
    d = xp2.shape[1]
    n_prompt_blocks = xp2.shape[0] // tm
    m = xp2.shape[0] + xs2.shape[0]
    rp_spec, rs_spec = _two_group_specs((tm, RET_WIDTH), n_prompt_blocks)
    pp_spec, ps_spec = _two_group_specs((tm, POOL_WIDTH), n_prompt_blocks)
    xp_spec, xs_spec = _two_group_specs((tm, d), n_prompt_blocks)
    return pl.pallas_call(
        functools.partial(_outproj_kernel, tm=tm, sub=ROW_SUB_BLOCK,
                          n_prompt_blocks=n_prompt_blocks),
        grid=(m // tm,),
        in_specs=[
            rp_spec, rs_spec, pp_spec, ps_spec, xp_spec, xs_spec,
            _resident(w_out_b.shape, lambda i: (0, 0)),
            pl.BlockSpec((1, d), lambda i: (0, 0)),
            pl.BlockSpec((1, d), lambda i: (0, 0)),
        ],
        out_specs=[
            pl.BlockSpec((tm, d), lambda i: (i, 0)),
            pl.BlockSpec((tm, d), lambda i: (i, 0)),
        ],
        out_shape=[
            jax.ShapeDtypeStruct((m, d), F32),
            jax.ShapeDtypeStruct((m, d), BF16),
        ],
        compiler_params=_compiler_params(("arbitrary",)),
        name="outproj",
    )(ret_p, ret_s, pool_p, pool_s, xp2, xs2, w_out_b,
      norm_post.reshape(1, d), norm_ffn_pre.reshape(1, d))


def _ffn_up_kernel(h_ref, wg_ref, wu_ref, a_ref, *, sub):
    h = h_ref[...]
    for c0 in range(0, a_ref.shape[1], sub):
        cs = slice(c0, c0 + sub)
        gate = jnp.dot(h, wg_ref[:, cs], preferred_element_type=F32)
        up = jnp.dot(h, wu_ref[:, cs], preferred_element_type=F32)
        a_ref[:, cs] = (_silu(gate) * up).astype(BF16)


def _ffn_up(h2, wg_b, wu_b, *, n_col_blocks):
    tm = ROW_BLOCK
    m, d = h2.shape
    dff = wg_b.shape[1]
    tf = dff // n_col_blocks
    assert tf % MXU_TILE == 0
    return pl.pallas_call(
        functools.partial(_ffn_up_kernel, sub=MXU_TILE),
        grid=(n_col_blocks, m // tm),
        in_specs=[
            pl.BlockSpec((tm, d), lambda j, i: (i, 0)),
            _resident((d, tf), lambda j, i: (0, j)),
            _resident((d, tf), lambda j, i: (0, j)),
        ],
        out_specs=pl.BlockSpec((tm, tf), lambda j, i: (i, j)),
        out_shape=jax.ShapeDtypeStruct((m, dff), BF16),
        compiler_params=_compiler_params(("arbitrary", "arbitrary")),
        name="ffn_up",
    )(h2, wg_b, wu_b)


def _ffn_down_kernel(a_ref, wd_ref, x1_ref, nw_ref, o_ref, *, tm, sub):
    for r0 in range(0, tm, sub):
        rs = slice(r0, r0 + sub)
        ff = jnp.dot(a_ref[rs, :], wd_ref[...], preferred_element_type=F32)
        o_ref[rs, :] = x1_ref[rs, :] + _rms_rows(ff, nw_ref[...])


def _ffn_down(act, x1, row0, rows, wd_b, norm_post):
    tm = ROW_BLOCK
    dff, d = wd_b.shape
    assert row0 % tm == 0 and rows % tm == 0
    block0 = row0 // tm
    return pl.pallas_call(
        functools.partial(_ffn_down_kernel, tm=tm, sub=MXU_TILE),
        grid=(rows // tm,),
        in_specs=[
            pl.BlockSpec((tm, dff), lambda i: (block0 + i, 0)),
            _resident((dff, d), lambda i: (0, 0)),
            pl.BlockSpec((tm, d), lambda i: (block0 + i, 0)),
            pl.BlockSpec((1, d), lambda i: (0, 0)),
        ],
        out_specs=pl.BlockSpec((tm, d), lambda i: (i, 0)),
        out_shape=jax.ShapeDtypeStruct((rows, d), F32),
        compiler_params=_compiler_params(("arbitrary",)),
        name="ffn_down",
    )(act, wd_b, x1, norm_post.reshape(1, d))


def _rope_tables(length, start):
    pos = start + jnp.arange(length, dtype=F32)
    inv_freq = 1.0 / (ROPE_BASE ** (jnp.arange(0, RET_DK, 2, dtype=F32) / RET_DK))
    ang = pos[:, None] * inv_freq[None, :]
    return jnp.cos(ang), jnp.sin(ang)


def kernel(x_prompt, x_sample, state_ret, state_pool, norm_mix_pre, norm_mix_post, w_in,
           ret_norm_w, w_pool, pool_scale, w_out, norm_ffn_pre, norm_ffn_post,
           w_gate, w_up, w_down):
    bp, lp, d = x_prompt.shape
    bs, ls, _ = x_sample.shape
    mp, ms = bp * lp, bs * ls
    xp2 = x_prompt.reshape(mp, d)
    xs2 = x_sample.reshape(ms, d)

    qkvg, u, (w_out_b, wg_b, wu_b, wd_b) = _inproj(
        xp2, xs2, norm_mix_pre, w_in.astype(BF16), (w_out, w_gate, w_up, w_down))

    s0_prompt = jnp.zeros((bp,) + state_ret.shape[1:], F32)
    buf_prompt = jnp.zeros((bp,) + state_pool.shape[1:], F32)
    ret_p, s_p = _retention(qkvg, 0, bp, lp, s0_prompt, 0.0, ret_norm_w, n_seq=1, n_chunks=2)
    pool_p, buf_p = _pool(u, 0, bp, lp, buf_prompt, w_pool, pool_scale, 0.0, bb=1, tl=512)
    ret_s, s_s = _retention(qkvg, mp, bs, ls, state_ret, float(PAST_LEN), ret_norm_w,
                            n_seq=8, n_chunks=1)
    pool_s, buf_s = _pool(u, mp, bs, ls, state_pool, w_pool, pool_scale, float(PAST_LEN),
                          bb=16, tl=ls)

    x1, h2 = _outproj(ret_p, ret_s, pool_p, pool_s, xp2, xs2, w_out_b,
                      norm_mix_post, norm_ffn_pre)
    act = _ffn_up(h2, wg_b, wu_b, n_col_blocks=2)
    yp = _ffn_down(act, x1, 0, mp, wd_b, norm_ffn_post)
    ys = _ffn_down(act, x1, mp, ms, wd_b, norm_ffn_post)

    return (yp.reshape(bp, lp, d), ys.reshape(bs, ls, d),
            s_p.astype(x_prompt.dtype), buf_p.astype(x_prompt.dtype),
            s_s.astype(state_ret.dtype), buf_s.astype(state_pool.dtype))
```

```python
import functools
import math

import jax
import jax.numpy as jnp
from jax import lax
from jax.experimental import pallas as pl
from jax.experimental.pallas import tpu as pltpu

F32 = jnp.float32
BF16 = jnp.bfloat16

RET_HEADS = 4
RET_DK = 256
RET_DV = 256
RET_WIDTH = RET_HEADS * RET_DV
RET_CHUNK = 128
ROPE_BASE = 10000.0
POOL_WINDOWS = (2, 4, 8, 16)
POOL_GC = 256
POOL_WIDTH = POOL_GC * len(POOL_WINDOWS)
POOL_BUF = max(POOL_WINDOWS) - 1
POOL_HIST = POOL_BUF + 1
PAST_LEN = 16384
EPS = 1e-6

ROW_BLOCK = 512
INPROJ_ROW_BLOCK = 256
ROW_SUB_BLOCK = 128
MXU_TILE = 256
VMEM_LIMIT_BYTES = 56 * 1024 * 1024


def _compiler_params(semantics):
    return pltpu.CompilerParams(dimension_semantics=semantics,
                                vmem_limit_bytes=VMEM_LIMIT_BYTES)


def _rms_rows(x, w):
    ms = jnp.mean(x * x, axis=-1, keepdims=True)
    return (x * lax.rsqrt(ms + EPS)) * w


def _silu(g):
    return g * (1.0 / (1.0 + jnp.exp(-g)))


def _resident(shape, index_map):
    return pl.BlockSpec(shape, index_map, pipeline_mode=pl.Buffered(1))


def _two_group_specs(block, n_prompt_blocks, n_sample_blocks, ahead=0):
    last_p = n_prompt_blocks - 1
    last_s = n_sample_blocks - 1
    return (pl.BlockSpec(block, lambda i: (jnp.minimum(i + ahead, last_p), 0)),
            pl.BlockSpec(block, lambda i: (jnp.clip(i + ahead - n_prompt_blocks, 0, last_s), 0)))


def _inproj_kernel(x0_ref, xp_ref, xs_ref, nw_ref, w_ref, *rest, tn, n_prompt_blocks, n_cast):
    cast_src = rest[:n_cast]
    o_ref, u_ref = rest[n_cast:n_cast + 2]
    cast_dst = rest[n_cast + 2:2 * n_cast + 2]
    h_ref = rest[-1]
    i = pl.program_id(0)
    slot = lax.rem(i, 2)

    @pl.when(i == 0)
    def _():
        h_ref[0] = _rms_rows(x0_ref[...], nw_ref[...]).astype(BF16)

    h = h_ref[slot]
    n_main = o_ref.shape[1]
    for c0 in range(0, w_ref.shape[1], tn):
        acc = jnp.dot(h, w_ref[:, c0:c0 + tn], preferred_element_type=F32)
        if c0 < n_main:
            o_ref[:, c0:c0 + tn] = acc.astype(BF16)
        else:
            u_ref[:, c0 - n_main:c0 - n_main + tn] = acc

    x_next = jnp.where(i + 1 < n_prompt_blocks, xp_ref[...], xs_ref[...])
    h_ref[1 - slot] = _rms_rows(x_next, nw_ref[...]).astype(BF16)

    for src, dst in zip(cast_src, cast_dst):
        dst[...] = src[...].astype(BF16)


def _inproj(xp2, xs2, norm_w, w_in_b, later_weights):
    tm = INPROJ_ROW_BLOCK
    d = xp2.shape[1]
    n = w_in_b.shape[1]
    n_main = n - POOL_WIDTH
    n_prompt_blocks = xp2.shape[0] // tm
    n_sample_blocks = xs2.shape[0] // tm
    m = xp2.shape[0] + xs2.shape[0]
    xp_spec, xs_spec = _two_group_specs((tm, d), n_prompt_blocks, n_sample_blocks, ahead=1)
    n_chunks = n_prompt_blocks
    cast_specs = [pl.BlockSpec((w.shape[0] // n_chunks, w.shape[1]),
                               lambda i: (jnp.minimum(i, n_chunks - 1), 0))
                  for w in later_weights]
    for w in later_weights:
        assert w.shape[0] % (16 * n_chunks) == 0
    outs = pl.pallas_call(
        functools.partial(_inproj_kernel, tn=2 * MXU_TILE,
                          n_prompt_blocks=n_prompt_blocks, n_cast=len(later_weights)),
        grid=(m // tm,),
        in_specs=[
            _resident((tm, d), lambda i: (0, 0)),
            xp_spec, xs_spec,
            pl.BlockSpec((1, d), lambda i: (0, 0)),
            _resident((d, n), lambda i: (0, 0)),
            *cast_specs,
        ],
        out_specs=[
            pl.BlockSpec((tm, n_main), lambda i: (i, 0)),
            pl.BlockSpec((tm, POOL_WIDTH), lambda i: (i, 0)),
            *cast_specs,
        ],
        out_shape=[
            jax.ShapeDtypeStruct((m, n_main), BF16),
            jax.ShapeDtypeStruct((m, POOL_WIDTH), F32),
            *[jax.ShapeDtypeStruct(w.shape, BF16) for w in later_weights],
        ],
        scratch_shapes=[pltpu.VMEM((2, tm, d), BF16)],
        compiler_params=_compiler_params(("arbitrary",)),
        name="inproj",
    )(xp2, xp2, xs2, norm_w.reshape(1, d), w_in_b, *later_weights)
    return outs[0], outs[1], outs[2:]


def _rope(x, cos, sin):
    half = x.shape[-1] // 2
    x1, x2 = x[:, :half], x[:, half:]
    return jnp.concatenate([x1 * cos - x2 * sin, x2 * cos + x1 * sin], axis=-1)


def _retention_kernel(q_ref, k_ref, v_ref, g_ref, s0_ref, cos_ref, sin_ref, cosk_ref, sink_ref,
                      decay_ref, nw_ref, o_ref, s_ref, *, n_seq, n_chunks, chunk, log_gammas,
                      single_step):
    if not single_step:
        @pl.when(pl.program_id(1) == 0)
        def _():
            s_ref[...] = s0_ref[...]

    pos = lax.broadcasted_iota(jnp.int32, (chunk, 1), 0).astype(F32)
    q_decs = [jnp.exp((pos + 1.0) * lg) for lg in log_gammas]
    k_decs = [jnp.exp((chunk - 1.0 - pos) * lg) for lg in log_gammas]
    items = [(si, h) for si in range(n_seq) for h in range(RET_HEADS)]

    def rows(si, ci):
        return slice((si * n_chunks + ci) * chunk, (si * n_chunks + ci + 1) * chunk)

    def cols(h):
        return slice(h * RET_DK, (h + 1) * RET_DK)

    for ci in range(n_chunks):
        ts = slice(ci * chunk, (ci + 1) * chunk)
        phase1 = []
        for si, h in items:
            qr = _rope(q_ref[rows(si, ci), cols(h)].astype(F32), cos_ref[ts, :], sin_ref[ts, :])
            kr = _rope(k_ref[rows(si, ci), cols(h)].astype(F32), cosk_ref[ts, :], sink_ref[ts, :])
            qb = qr.astype(BF16)
            scores = lax.dot_general(qb, kr.astype(BF16), (((1,), (1,)), ((), ())),
                                     preferred_element_type=F32) * decay_ref[h]
            phase1.append((qb, kr, scores.astype(BF16)))
        for (si, h), (qb, kr, scores_b) in zip(items, phase1):
            rs, cs = rows(si, ci), cols(h)
            vb = v_ref[rs, cs]
            s_old = s0_ref[si, h] if (single_step and ci == 0) else s_ref[si, h]
            o = jnp.dot(scores_b, vb, preferred_element_type=F32)
            o = o + jnp.dot(qb, s_old.astype(BF16), preferred_element_type=F32) * q_decs[h]
            kd = (kr * k_decs[h]).astype(BF16)
            s_ref[si, h] = math.exp(chunk * log_gammas[h]) * s_old + lax.dot_general(
                kd, vb, (((0,), (0,)), ((), ())), preferred_element_type=F32)
            o = o * lax.rsqrt(jnp.mean(o * o, axis=-1, keepdims=True) + EPS)
            o = o * nw_ref[:, cs]
            o_ref[rs, cs] = (_silu(g_ref[rs, cs].astype(F32)) * o).astype(BF16)


def _retention(qkvg, row0, b, l, s0, start, ret_norm_w, *, n_seq, n_chunks):
    chunk = RET_CHUNK if l % RET_CHUNK == 0 else l
    log_gammas = tuple(math.log(1.0 - 2.0 ** (-5.0 - h)) for h in range(RET_HEADS))
    half = RET_DK // 2
    steps_per_seq = l // (chunk * n_chunks)
    block_rows = n_seq * n_chunks * chunk
    assert row0 % block_rows == 0 and (n_seq == 1 or steps_per_seq == 1)
    block0 = row0 // block_rows

    cos, sin = _rope_tables(l, start)
    k_scale = RET_DK ** -0.5
    idx = jnp.arange(chunk, dtype=F32)
    diff = idx[:, None] - idx[None, :]
    lg = jnp.asarray(log_gammas, F32)[:, None, None]
    decay = jnp.where(diff[None] >= 0, jnp.exp(jnp.maximum(diff[None], 0.0) * lg), 0.0)

    def col_block(idx):
        return pl.BlockSpec((block_rows, RET_WIDTH),
                            lambda i, c: (block0 + i * steps_per_seq + c, idx))

    state_spec = pl.BlockSpec((n_seq, RET_HEADS, RET_DK, RET_DV), lambda i, c: (i, 0, 0, 0))
    table_spec = pl.BlockSpec((n_chunks * chunk, half), lambda i, c: (c, 0))
    return pl.pallas_call(
        functools.partial(_retention_kernel, n_seq=n_seq, n_chunks=n_chunks, chunk=chunk,
                          log_gammas=log_gammas, single_step=steps_per_seq == 1),
        grid=(b // n_seq, steps_per_seq),
        in_specs=[
            col_block(0), col_block(1), col_block(2), col_block(3),
            state_spec,
            table_spec, table_spec, table_spec, table_spec,
            pl.BlockSpec((RET_HEADS, chunk, chunk), lambda i, c: (0, 0, 0)),
            pl.BlockSpec((1, RET_WIDTH), lambda i, c: (0, 0)),
        ],
        out_specs=[
            pl.BlockSpec((block_rows, RET_WIDTH), lambda i, c: (i * steps_per_seq + c, 0)),
            state_spec,
        ],
        out_shape=[
            jax.ShapeDtypeStruct((b * l, RET_WIDTH), BF16),
            jax.ShapeDtypeStruct((b, RET_HEADS, RET_DK, RET_DV), F32),
        ],
        compiler_params=_compiler_params(("parallel", "arbitrary")),
        name="retention",
    )(qkvg, qkvg, qkvg, qkvg, s0, cos, sin, cos * k_scale, sin * k_scale, decay,
      ret_norm_w.reshape(1, RET_WIDTH))


def _pool_kernel(u_ref, buf_ref, wp_ref, ps_ref, o_ref, nb_ref, xp_ref, *, bb, tl, start):
    t = pl.program_id(1)

    @pl.when(t == 0)
    def _():
        xp_ref[:, 0:1, :] = jnp.zeros((bb, 1, POOL_WIDTH), F32)
        xp_ref[:, 1:POOL_HIST, :] = buf_ref[...]

    xp_ref[:, POOL_HIST:POOL_HIST + tl, :] = u_ref[...].reshape(bb, tl, POOL_WIDTH)
    pos = (lax.broadcasted_iota(jnp.int32, (bb, tl, POOL_GC), 1) + t * tl).astype(F32) + start
    rows = POOL_HIST + tl

    for g, w in enumerate(POOL_WINDOWS):
        cs = slice(g * POOL_GC, (g + 1) * POOL_GC)
        u = xp_ref[:, POOL_HIST:POOL_HIST + tl, cs]
        acc = xp_ref[:, :, cs].reshape(bb * rows, POOL_GC)
        shift = 1
        while shift < w:
            acc = acc + pltpu.roll(acc, shift, 0)
            shift *= 2
        acc = acc.reshape(bb, rows, POOL_GC)[:, POOL_HIST:, :]
        cnt = jnp.minimum(pos + 1.0, float(w))
        pooled = (acc / cnt - u).astype(BF16).reshape(bb * tl, POOL_GC)
        y = jnp.dot(pooled, wp_ref[g].astype(BF16), preferred_element_type=F32)
        o_ref[:, cs] = (y * ps_ref[:, cs]).astype(BF16)

    nb_ref[...] = xp_ref[:, tl + 1:tl + POOL_HIST, :]
    if tl >= POOL_HIST:
        xp_ref[:, 0:POOL_HIST, :] = xp_ref[:, tl:tl + POOL_HIST, :]


def _pool(u_all, row0, b, l, buf, w_pool, pool_scale, start, *, bb, tl):
    assert l == tl or (bb == 1 and tl >= POOL_HIST)
    steps_per_seq = l // tl
    block_rows = bb * tl
    assert row0 % block_rows == 0
    block0 = row0 // block_rows
    return pl.pallas_call(
        functools.partial(_pool_kernel, bb=bb, tl=tl, start=float(start)),
        grid=(b // bb, steps_per_seq),
        in_specs=[
            pl.BlockSpec((block_rows, POOL_WIDTH),
                         lambda i, t: (block0 + i * steps_per_seq + t, 0)),
            pl.BlockSpec((bb, POOL_BUF, POOL_WIDTH), lambda i, t: (i, 0, 0)),
            pl.BlockSpec(w_pool.shape, lambda i, t: (0, 0, 0)),
            pl.BlockSpec((1, POOL_WIDTH), lambda i, t: (0, 0)),
        ],
        out_specs=[
            pl.BlockSpec((block_rows, POOL_WIDTH), lambda i, t: (i * steps_per_seq + t, 0)),
            pl.BlockSpec((bb, POOL_BUF, POOL_WIDTH), lambda i, t: (i, 0, 0)),
        ],
        out_shape=[
            jax.ShapeDtypeStruct((b * l, POOL_WIDTH), BF16),
            jax.ShapeDtypeStruct((b, POOL_BUF, POOL_WIDTH), F32),
        ],
        scratch_shapes=[pltpu.VMEM((bb, POOL_HIST + tl, POOL_WIDTH), F32)],
        compiler_params=_compiler_params(("parallel", "arbitrary")),
        name="pool",
    )(u_all, buf, w_pool, pool_scale.reshape(1, POOL_WIDTH))


def _outproj_kernel(rp_ref, rs_ref, pp_ref, ps_ref, xp_ref, xs_ref, w_ref, nw1_ref, nw2_ref,
                    x1_ref, h2_ref, *, tm, sub, n_prompt_blocks):
    is_prompt = pl.program_id(0) < n_prompt_blocks
    for r0 in range(0, tm, sub):
        rs = slice(r0, r0 + sub)
        ret = jnp.where(is_prompt, rp_ref[rs, :], rs_ref[rs, :])
        pool = jnp.where(is_prompt, pp_ref[rs, :], ps_ref[rs, :])
        mix = jnp.dot(ret, w_ref[0:RET_WIDTH, :], preferred_element_type=F32)
        mix = mix + jnp.dot(pool, w_ref[RET_WIDTH:, :], preferred_element_type=F32)
        x1 = jnp.where(is_prompt, xp_ref[rs, :], xs_ref[rs, :]) + _rms_rows(mix, nw1_ref[...])
        x1_ref[rs, :] = x1
        h2_ref[rs, :] = _rms_rows(x1, nw2_ref[...]).astype(BF16)


def _outproj(ret_p, ret_s, pool_p, pool_s, xp2, xs2, w_out_b, norm_post, norm_ffn_pre):
    tm = ROW_BLOCK
    d = xp2.shape[1]
    n_prompt_blocks = xp2.shape[0] // tm
    n_sample_blocks = xs2.shape[0] // tm
    m = xp2.shape[0] + xs2.shape[0]
    rp_spec, rs_spec = _two_group_specs((tm, RET_WIDTH), n_prompt_blocks, n_sample_blocks)
    pp_spec, ps_spec = _two_group_specs((tm, POOL_WIDTH), n_prompt_blocks, n_sample_blocks)
    xp_spec, xs_spec = _two_group_specs((tm, d), n_prompt_blocks, n_sample_blocks)
    return pl.pallas_call(
        functools.partial(_outproj_kernel, tm=tm, sub=ROW_SUB_BLOCK,
                          n_prompt_blocks=n_prompt_blocks),
        grid=(m // tm,),
        in_specs=[
            rp_spec, rs_spec, pp_spec, ps_spec, xp_spec, xs_spec,
            _resident(w_out_b.shape, lambda i: (0, 0)),
            pl.BlockSpec((1, d), lambda i: (0, 0)),
            pl.BlockSpec((1, d), lambda i: (0, 0)),
        ],
        out_specs=[
            pl.BlockSpec((tm, d), lambda i: (i, 0)),
            pl.BlockSpec((tm, d), lambda i: (i, 0)),
        ],
        out_shape=[
            jax.ShapeDtypeStruct((m, d), F32),
            jax.ShapeDtypeStruct((m, d), BF16),
        ],
        compiler_params=_compiler_params(("arbitrary",)),
        name="outproj",
    )(ret_p, ret_s, pool_p, pool_s, xp2, xs2, w_out_b,
      norm_post.reshape(1, d), norm_ffn_pre.reshape(1, d))


def _ffn_up_kernel(h_ref, wg_ref, wu_ref, a_ref, *, sub):
    h = h_ref[...]
    for c0 in range(0, a_ref.shape[1], sub):
        cs = slice(c0, c0 + sub)
        gate = jnp.dot(h, wg_ref[:, cs], preferred_element_type=F32)
        up = jnp.dot(h, wu_ref[:, cs], preferred_element_type=F32)
        a_ref[:, cs] = (_silu(gate) * up).astype(BF16)


def _ffn_up(h2, wg_b, wu_b, *, n_col_blocks):
    tm = 2 * ROW_BLOCK
    m, d = h2.shape
    assert m % tm == 0
    dff = wg_b.shape[1]
    tf = dff // n_col_blocks
    assert tf % MXU_TILE == 0
    return pl.pallas_call(
        functools.partial(_ffn_up_kernel, sub=MXU_TILE),
        grid=(n_col_blocks, m // tm),
        in_specs=[
            pl.BlockSpec((tm, d), lambda j, i: (i, 0)),
            _resident((d, tf), lambda j, i: (0, j)),
            _resident((d, tf), lambda j, i: (0, j)),
        ],
        out_specs=pl.BlockSpec((tm, tf), lambda j, i: (i, j)),
        out_shape=jax.ShapeDtypeStruct((m, dff), BF16),
        compiler_params=_compiler_params(("arbitrary", "arbitrary")),
        name="ffn_up",
    )(h2, wg_b, wu_b)


def _ffn_down_kernel(a_ref, wd_ref, x1_ref, nw_ref, o_ref, *, tm, sub):
    for r0 in range(0, tm, sub):
        rs = slice(r0, r0 + sub)
        ff = jnp.dot(a_ref[rs, :], wd_ref[...], preferred_element_type=F32)
        o_ref[rs, :] = x1_ref[rs, :] + _rms_rows(ff, nw_ref[...])


def _ffn_down(act, x1, row0, rows, wd_b, norm_post):
    tm = ROW_BLOCK
    dff, d = wd_b.shape
    assert row0 % tm == 0 and rows % tm == 0
    block0 = row0 // tm
    return pl.pallas_call(
        functools.partial(_ffn_down_kernel, tm=tm, sub=MXU_TILE),
        grid=(rows // tm,),
        in_specs=[
            pl.BlockSpec((tm, dff), lambda i: (block0 + i, 0)),
            _resident((dff, d), lambda i: (0, 0)),
            pl.BlockSpec((tm, d), lambda i: (block0 + i, 0)),
            pl.BlockSpec((1, d), lambda i: (0, 0)),
        ],
        out_specs=pl.BlockSpec((tm, d), lambda i: (i, 0)),
        out_shape=jax.ShapeDtypeStruct((rows, d), F32),
        compiler_params=_compiler_params(("arbitrary",)),
        name="ffn_down",
    )(act, wd_b, x1, norm_post.reshape(1, d))


def _rope_tables(length, start):
    pos = start + jnp.arange(length, dtype=F32)
    inv_freq = 1.0 / (ROPE_BASE ** (jnp.arange(0, RET_DK, 2, dtype=F32) / RET_DK))
    ang = pos[:, None] * inv_freq[None, :]
    return jnp.cos(ang), jnp.sin(ang)


def kernel(x_prompt, x_sample, state_ret, state_pool, norm_mix_pre, norm_mix_post, w_in,
           ret_norm_w, w_pool, pool_scale, w_out, norm_ffn_pre, norm_ffn_post,
           w_gate, w_up, w_down):
    bp, lp, d = x_prompt.shape
    bs, ls, _ = x_sample.shape
    mp, ms = bp * lp, bs * ls
    xp2 = x_prompt.reshape(mp, d)
    xs2 = x_sample.reshape(ms, d)

    qkvg, u, (w_out_b, wg_b, wu_b, wd_b) = _inproj(
        xp2, xs2, norm_mix_pre, w_in.astype(BF16), (w_out, w_gate, w_up, w_down))

    s0_prompt = jnp.zeros((bp,) + state_ret.shape[1:], F32)
    buf_prompt = jnp.zeros((bp,) + state_pool.shape[1:], F32)
    ret_p, s_p = _retention(qkvg, 0, bp, lp, s0_prompt, 0.0, ret_norm_w, n_seq=1, n_chunks=2)
    pool_p, buf_p = _pool(u, 0, bp, lp, buf_prompt, w_pool, pool_scale, 0.0, bb=1, tl=512)
    ret_s, s_s = _retention(qkvg, mp, bs, ls, state_ret, float(PAST_LEN), ret_norm_w,
                            n_seq=8, n_chunks=1)
    pool_s, buf_s = _pool(u, mp, bs, ls, state_pool, w_pool, pool_scale, float(PAST_LEN),
                          bb=16, tl=ls)

    x1, h2 = _outproj(ret_p, ret_s, pool_p, pool_s, xp2, xs2, w_out_b,
                      norm_mix_post, norm_ffn_pre)
    act = _ffn_up(h2, wg_b, wu_b, n_col_blocks=2)
    yp = _ffn_down(act, x1, 0, mp, wd_b, norm_ffn_post)
    ys = _ffn_down(act, x1, mp, ms, wd_b, norm_ffn_post)

    return (yp.reshape(bp, lp, d), ys.reshape(bs, ls, d),
            s_p.astype(x_prompt.dtype), buf_p.astype(x_prompt.dtype),
            s_s.astype(state_ret.dtype), buf_s.astype(state_pool.dtype))
```

```python
import functools
import math

import jax
import jax.numpy as jnp
import numpy as np
from jax import lax
from jax.experimental import pallas as pl
from jax.experimental.pallas import tpu as pltpu

F32 = jnp.float32
BF16 = jnp.bfloat16

RET_HEADS = 4
RET_DK = 256
RET_DV = 256
RET_WIDTH = RET_HEADS * RET_DV
RET_CHUNK = 128
ROPE_BASE = 10000.0
POOL_WINDOWS = (2, 4, 8, 16)
POOL_GC = 256
POOL_WIDTH = POOL_GC * len(POOL_WINDOWS)
POOL_BUF = max(POOL_WINDOWS) - 1
POOL_HIST = POOL_BUF + 1
PAST_LEN = 16384
EPS = 1e-6

ROW_BLOCK = 512
INPROJ_ROW_BLOCK = 256
ROW_SUB_BLOCK = 128
MXU_TILE = 256
VMEM_LIMIT_BYTES = 56 * 1024 * 1024
FFN_DOWN_VMEM_LIMIT_BYTES = 62 * 1024 * 1024


def _compiler_params(semantics):
    return pltpu.CompilerParams(dimension_semantics=semantics,
                                vmem_limit_bytes=VMEM_LIMIT_BYTES)


def _rms_rows(x, w):
    ms = jnp.mean(x * x, axis=-1, keepdims=True)
    return (x * lax.rsqrt(ms + EPS)) * w


def _silu(g):
    return g * (1.0 / (1.0 + jnp.exp(-g)))


def _resident(shape, index_map):
    return pl.BlockSpec(shape, index_map, pipeline_mode=pl.Buffered(1))


def _two_group_specs(block, n_prompt_blocks, n_sample_blocks, ahead=0):
    last_p = n_prompt_blocks - 1
    last_s = n_sample_blocks - 1
    return (pl.BlockSpec(block, lambda i: (jnp.minimum(i + ahead, last_p), 0)),
            pl.BlockSpec(block, lambda i: (jnp.clip(i + ahead - n_prompt_blocks, 0, last_s), 0)))


def _inproj_kernel(x0_ref, xp_ref, xs_ref, nw_ref, w_ref, *rest, tn, n_prompt_blocks, n_cast):
    cast_src = rest[:n_cast]
    o_ref, u_ref = rest[n_cast:n_cast + 2]
    cast_dst = rest[n_cast + 2:2 * n_cast + 2]
    h_ref = rest[-1]
    i = pl.program_id(0)
    slot = lax.rem(i, 2)

    @pl.when(i == 0)
    def _():
        h_ref[0] = _rms_rows(x0_ref[...], nw_ref[...]).astype(BF16)

    h = h_ref[slot]
    n_main = o_ref.shape[1]
    for c0 in range(0, w_ref.shape[1], tn):
        acc = jnp.dot(h, w_ref[:, c0:c0 + tn], preferred_element_type=F32)
        if c0 < n_main:
            o_ref[:, c0:c0 + tn] = acc.astype(BF16)
        else:
            u_ref[:, c0 - n_main:c0 - n_main + tn] = acc

    x_next = jnp.where(i + 1 < n_prompt_blocks, xp_ref[...], xs_ref[...])
    h_ref[1 - slot] = _rms_rows(x_next, nw_ref[...]).astype(BF16)

    for src, dst in zip(cast_src, cast_dst):
        dst[...] = src[...].astype(BF16)


def _inproj(xp2, xs2, norm_w, w_in_b, later_weights):
    tm = INPROJ_ROW_BLOCK
    d = xp2.shape[1]
    n = w_in_b.shape[1]
    n_main = n - POOL_WIDTH
    n_prompt_blocks = xp2.shape[0] // tm
    n_sample_blocks = xs2.shape[0] // tm
    m = xp2.shape[0] + xs2.shape[0]
    xp_spec, xs_spec = _two_group_specs((tm, d), n_prompt_blocks, n_sample_blocks, ahead=1)
    n_chunks = n_prompt_blocks
    cast_specs = [pl.BlockSpec((w.shape[0] // n_chunks, w.shape[1]),
                               lambda i: (jnp.minimum(i, n_chunks - 1), 0))
                  for w in later_weights]
    for w in later_weights:
        assert w.shape[0] % (16 * n_chunks) == 0
    outs = pl.pallas_call(
        functools.partial(_inproj_kernel, tn=2 * MXU_TILE,
                          n_prompt_blocks=n_prompt_blocks, n_cast=len(later_weights)),
        grid=(m // tm,),
        in_specs=[
            _resident((tm, d), lambda i: (0, 0)),
            xp_spec, xs_spec,
            pl.BlockSpec((1, d), lambda i: (0, 0)),
            _resident((d, n), lambda i: (0, 0)),
            *cast_specs,
        ],
        out_specs=[
            pl.BlockSpec((tm, n_main), lambda i: (i, 0)),
            pl.BlockSpec((tm, POOL_WIDTH), lambda i: (i, 0)),
            *cast_specs,
        ],
        out_shape=[
            jax.ShapeDtypeStruct((m, n_main), BF16),
            jax.ShapeDtypeStruct((m, POOL_WIDTH), F32),
            *[jax.ShapeDtypeStruct(w.shape, BF16) for w in later_weights],
        ],
        scratch_shapes=[pltpu.VMEM((2, tm, d), BF16)],
        compiler_params=_compiler_params(("arbitrary",)),
        name="inproj",
    )(xp2, xp2, xs2, norm_w.reshape(1, d), w_in_b, *later_weights)
    return outs[0], outs[1], outs[2:]


def _rope(x, cos, sin):
    half = x.shape[-1] // 2
    x1, x2 = x[:, :half], x[:, half:]
    return jnp.concatenate([x1 * cos - x2 * sin, x2 * cos + x1 * sin], axis=-1)


def _retention_kernel(q_ref, k_ref, v_ref, g_ref, s0_ref, cos_ref, sin_ref, cosk_ref, sink_ref,
                      decay_ref, nw_ref, o_ref, s_ref, *, n_seq, n_chunks, chunk, log_gammas,
                      single_step):
    if not single_step:
        @pl.when(pl.program_id(1) == 0)
        def _():
            s_ref[...] = s0_ref[...]

    pos = lax.broadcasted_iota(jnp.int32, (chunk, 1), 0).astype(F32)
    q_decs = [jnp.exp((pos + 1.0) * lg) for lg in log_gammas]
    k_decs = [jnp.exp((chunk - 1.0 - pos) * lg) for lg in log_gammas]
    items = [(si, h) for si in range(n_seq) for h in range(RET_HEADS)]

    def rows(si, ci):
        return slice((si * n_chunks + ci) * chunk, (si * n_chunks + ci + 1) * chunk)

    def cols(h):
        return slice(h * RET_DK, (h + 1) * RET_DK)

    for ci in range(n_chunks):
        ts = slice(ci * chunk, (ci + 1) * chunk)
        phase1 = []
        for si, h in items:
            qr = _rope(q_ref[rows(si, ci), cols(h)].astype(F32), cos_ref[ts, :], sin_ref[ts, :])
            kr = _rope(k_ref[rows(si, ci), cols(h)].astype(F32), cosk_ref[ts, :], sink_ref[ts, :])
            qb = qr.astype(BF16)
            scores = lax.dot_general(qb, kr.astype(BF16), (((1,), (1,)), ((), ())),
                                     preferred_element_type=F32) * decay_ref[h]
            phase1.append((qb, kr, scores.astype(BF16)))
        for (si, h), (qb, kr, scores_b) in zip(items, phase1):
            rs, cs = rows(si, ci), cols(h)
            vb = v_ref[rs, cs]
            s_old = s0_ref[si, h] if (single_step and ci == 0) else s_ref[si, h]
            o = jnp.dot(scores_b, vb, preferred_element_type=F32)
            o = o + jnp.dot(qb, s_old.astype(BF16), preferred_element_type=F32) * q_decs[h]
            kd = (kr * k_decs[h]).astype(BF16)
            s_ref[si, h] = math.exp(chunk * log_gammas[h]) * s_old + lax.dot_general(
                kd, vb, (((0,), (0,)), ((), ())), preferred_element_type=F32)
            o = o * lax.rsqrt(jnp.mean(o * o, axis=-1, keepdims=True) + EPS)
            o = o * nw_ref[:, cs]
            o_ref[rs, cs] = (_silu(g_ref[rs, cs].astype(F32)) * o).astype(BF16)


def _retention(qkvg, row0, b, l, s0, start, ret_norm_w, *, n_seq, n_chunks):
    chunk = RET_CHUNK if l % RET_CHUNK == 0 else l
    log_gammas = tuple(math.log(1.0 - 2.0 ** (-5.0 - h)) for h in range(RET_HEADS))
    half = RET_DK // 2
    steps_per_seq = l // (chunk * n_chunks)
    block_rows = n_seq * n_chunks * chunk
    assert row0 % block_rows == 0 and (n_seq == 1 or steps_per_seq == 1)
    block0 = row0 // block_rows

    cos, sin = _rope_tables(l, start)
    k_scale = RET_DK ** -0.5
    idx = np.arange(chunk, dtype=np.float64)
    diff = idx[:, None] - idx[None, :]
    lg = np.asarray(log_gammas, np.float64)[:, None, None]
    decay = np.where(diff[None] >= 0, np.exp(np.maximum(diff[None], 0.0) * lg), 0.0)
    tables = [jnp.asarray(t, F32) for t in (cos, sin, cos * k_scale, sin * k_scale, decay)]

    def col_block(idx):
        return pl.BlockSpec((block_rows, RET_WIDTH),
                            lambda i, c: (block0 + i * steps_per_seq + c, idx))

    state_spec = pl.BlockSpec((n_seq, RET_HEADS, RET_DK, RET_DV), lambda i, c: (i, 0, 0, 0))
    table_spec = pl.BlockSpec((n_chunks * chunk, half), lambda i, c: (c, 0))
    return pl.pallas_call(
        functools.partial(_retention_kernel, n_seq=n_seq, n_chunks=n_chunks, chunk=chunk,
                          log_gammas=log_gammas, single_step=steps_per_seq == 1),
        grid=(b // n_seq, steps_per_seq),
        in_specs=[
            col_block(0), col_block(1), col_block(2), col_block(3),
            state_spec,
            table_spec, table_spec, table_spec, table_spec,
            pl.BlockSpec((RET_HEADS, chunk, chunk), lambda i, c: (0, 0, 0)),
            pl.BlockSpec((1, RET_WIDTH), lambda i, c: (0, 0)),
        ],
        out_specs=[
            pl.BlockSpec((block_rows, RET_WIDTH), lambda i, c: (i * steps_per_seq + c, 0)),
            state_spec,
        ],
        out_shape=[
            jax.ShapeDtypeStruct((b * l, RET_WIDTH), BF16),
            jax.ShapeDtypeStruct((b, RET_HEADS, RET_DK, RET_DV), F32),
        ],
        compiler_params=_compiler_params(("parallel", "arbitrary")),
        name="retention",
    )(qkvg, qkvg, qkvg, qkvg, s0, *tables, ret_norm_w.reshape(1, RET_WIDTH))


def _pool_kernel(u_ref, buf_ref, wp_ref, ps_ref, o_ref, nb_ref, xp_ref, *, bb, tl, start):
    t = pl.program_id(1)

    @pl.when(t == 0)
    def _():
        xp_ref[:, 0:1, :] = jnp.zeros((bb, 1, POOL_WIDTH), F32)
        xp_ref[:, 1:POOL_HIST, :] = buf_ref[...]

    xp_ref[:, POOL_HIST:POOL_HIST + tl, :] = u_ref[...].reshape(bb, tl, POOL_WIDTH)
    pos = (lax.broadcasted_iota(jnp.int32, (bb, tl, POOL_GC), 1) + t * tl).astype(F32) + start
    rows = POOL_HIST + tl

    for g, w in enumerate(POOL_WINDOWS):
        cs = slice(g * POOL_GC, (g + 1) * POOL_GC)
        u = xp_ref[:, POOL_HIST:POOL_HIST + tl, cs]
        acc = xp_ref[:, :, cs].reshape(bb * rows, POOL_GC)
        shift = 1
        while shift < w:
            acc = acc + pltpu.roll(acc, shift, 0)
            shift *= 2
        acc = acc.reshape(bb, rows, POOL_GC)[:, POOL_HIST:, :]
        cnt = jnp.minimum(pos + 1.0, float(w))
        pooled = (acc / cnt - u).astype(BF16).reshape(bb * tl, POOL_GC)
        y = jnp.dot(pooled, wp_ref[g].astype(BF16), preferred_element_type=F32)
        o_ref[:, cs] = (y * ps_ref[:, cs]).astype(BF16)

    nb_ref[...] = xp_ref[:, tl + 1:tl + POOL_HIST, :]
    if tl >= POOL_HIST:
        xp_ref[:, 0:POOL_HIST, :] = xp_ref[:, tl:tl + POOL_HIST, :]


def _pool(u_all, row0, b, l, buf, w_pool, pool_scale, start, *, bb, tl):
    assert l == tl or (bb == 1 and tl >= POOL_HIST)
    steps_per_seq = l // tl
    block_rows = bb * tl
    assert row0 % block_rows == 0
    block0 = row0 // block_rows
    return pl.pallas_call(
        functools.partial(_pool_kernel, bb=bb, tl=tl, start=float(start)),
        grid=(b // bb, steps_per_seq),
        in_specs=[
            pl.BlockSpec((block_rows, POOL_WIDTH),
                         lambda i, t: (block0 + i * steps_per_seq + t, 0)),
            pl.BlockSpec((bb, POOL_BUF, POOL_WIDTH), lambda i, t: (i, 0, 0)),
            pl.BlockSpec(w_pool.shape, lambda i, t: (0, 0, 0)),
            pl.BlockSpec((1, POOL_WIDTH), lambda i, t: (0, 0)),
        ],
        out_specs=[
            pl.BlockSpec((block_rows, POOL_WIDTH), lambda i, t: (i * steps_per_seq + t, 0)),
            pl.BlockSpec((bb, POOL_BUF, POOL_WIDTH), lambda i, t: (i, 0, 0)),
        ],
        out_shape=[
            jax.ShapeDtypeStruct((b * l, POOL_WIDTH), BF16),
            jax.ShapeDtypeStruct((b, POOL_BUF, POOL_WIDTH), F32),
        ],
        scratch_shapes=[pltpu.VMEM((bb, POOL_HIST + tl, POOL_WIDTH), F32)],
        compiler_params=_compiler_params(("parallel", "arbitrary")),
        name="pool",
    )(u_all, buf, w_pool, pool_scale.reshape(1, POOL_WIDTH))


def _outproj_kernel(rp_ref, rs_ref, pp_ref, ps_ref, xp_ref, xs_ref, w_ref, nw1_ref, nw2_ref,
                    x1_ref, h2_ref, *, tm, sub, n_prompt_blocks):
    is_prompt = pl.program_id(0) < n_prompt_blocks
    for r0 in range(0, tm, sub):
        rs = slice(r0, r0 + sub)
        ret = jnp.where(is_prompt, rp_ref[rs, :], rs_ref[rs, :])
        pool = jnp.where(is_prompt, pp_ref[rs, :], ps_ref[rs, :])
        mix = jnp.dot(ret, w_ref[0:RET_WIDTH, :], preferred_element_type=F32)
        mix = mix + jnp.dot(pool, w_ref[RET_WIDTH:, :], preferred_element_type=F32)
        x1 = jnp.where(is_prompt, xp_ref[rs, :], xs_ref[rs, :]) + _rms_rows(mix, nw1_ref[...])
        x1_ref[rs, :] = x1
        h2_ref[rs, :] = _rms_rows(x1, nw2_ref[...]).astype(BF16)


def _outproj(ret_p, ret_s, pool_p, pool_s, xp2, xs2, w_out_b, norm_post, norm_ffn_pre):
    tm = ROW_BLOCK
    d = xp2.shape[1]
    n_prompt_blocks = xp2.shape[0] // tm
    n_sample_blocks = xs2.shape[0] // tm
    m = xp2.shape[0] + xs2.shape[0]
    rp_spec, rs_spec = _two_group_specs((tm, RET_WIDTH), n_prompt_blocks, n_sample_blocks)
    pp_spec, ps_spec = _two_group_specs((tm, POOL_WIDTH), n_prompt_blocks, n_sample_blocks)
    xp_spec, xs_spec = _two_group_specs((tm, d), n_prompt_blocks, n_sample_blocks)
    return pl.pallas_call(
        functools.partial(_outproj_kernel, tm=tm, sub=ROW_SUB_BLOCK,
                          n_prompt_blocks=n_prompt_blocks),
        grid=(m // tm,),
        in_specs=[
            rp_spec, rs_spec, pp_spec, ps_spec, xp_spec, xs_spec,
            _resident(w_out_b.shape, lambda i: (0, 0)),
            pl.BlockSpec((1, d), lambda i: (0, 0)),
            pl.BlockSpec((1, d), lambda i: (0, 0)),
        ],
        out_specs=[
            pl.BlockSpec((tm, d), lambda i: (i, 0)),
            pl.BlockSpec((tm, d), lambda i: (i, 0)),
        ],
        out_shape=[
            jax.ShapeDtypeStruct((m, d), F32),
            jax.ShapeDtypeStruct((m, d), BF16),
        ],
        compiler_params=_compiler_params(("arbitrary",)),
        name="outproj",
    )(ret_p, ret_s, pool_p, pool_s, xp2, xs2, w_out_b,
      norm_post.reshape(1, d), norm_ffn_pre.reshape(1, d))


def _ffn_up_kernel(h_ref, wg_ref, wu_ref, a_ref, *, sub):
    h = h_ref[...]
    for c0 in range(0, a_ref.shape[1], sub):
        cs = slice(c0, c0 + sub)
        gate = jnp.dot(h, wg_ref[:, cs], preferred_element_type=F32)
        up = jnp.dot(h, wu_ref[:, cs], preferred_element_type=F32)
        a_ref[:, cs] = (_silu(gate) * up).astype(BF16)


def _ffn_up(h2, wg_b, wu_b, *, n_col_blocks):
    tm = 2 * ROW_BLOCK
    m, d = h2.shape
    assert m % tm == 0
    dff = wg_b.shape[1]
    tf = dff // n_col_blocks
    assert tf % MXU_TILE == 0
    return pl.pallas_call(
        functools.partial(_ffn_up_kernel, sub=MXU_TILE),
        grid=(n_col_blocks, m // tm),
        in_specs=[
            pl.BlockSpec((tm, d), lambda j, i: (i, 0)),
            _resident((d, tf), lambda j, i: (0, j)),
            _resident((d, tf), lambda j, i: (0, j)),
        ],
        out_specs=pl.BlockSpec((tm, tf), lambda j, i: (i, j)),
        out_shape=jax.ShapeDtypeStruct((m, dff), BF16),
        compiler_params=_compiler_params(("arbitrary", "arbitrary")),
        name="ffn_up",
    )(h2, wg_b, wu_b)


def _ffn_down_kernel(a_ref, wd_ref, x1_ref, nw_ref, ys_ref, yp_ref, *, tm, sub,
                     n_sample_blocks):
    i = pl.program_id(0)
    is_sample = i < n_sample_blocks

    @pl.when(i == 0)
    def _():
        yp_ref[...] = jnp.zeros(yp_ref.shape, F32)

    for r0 in range(0, tm, sub):
        rs = slice(r0, r0 + sub)
        ff = jnp.dot(a_ref[rs, :], wd_ref[...], preferred_element_type=F32)
        y = x1_ref[rs, :] + _rms_rows(ff, nw_ref[...])
        ys_ref[rs, :] = jnp.where(is_sample, y, ys_ref[rs, :])
        yp_ref[rs, :] = jnp.where(is_sample, yp_ref[rs, :], y)


def _ffn_down(act, x1, rows_prompt, rows_sample, wd_b, norm_post):
    tm = ROW_BLOCK
    dff, d = wd_b.shape
    n_prompt_blocks = rows_prompt // tm
    n_sample_blocks = rows_sample // tm

    def in_block(i):
        return jnp.where(i < n_sample_blocks, n_prompt_blocks + i, i - n_sample_blocks)

    ys, yp = pl.pallas_call(
        functools.partial(_ffn_down_kernel, tm=tm, sub=MXU_TILE,
                          n_sample_blocks=n_sample_blocks),
        grid=(n_prompt_blocks + n_sample_blocks,),
        in_specs=[
            pl.BlockSpec((tm, dff), lambda i: (in_block(i), 0)),
            _resident((dff, d), lambda i: (0, 0)),
            pl.BlockSpec((tm, d), lambda i: (in_block(i), 0)),
            pl.BlockSpec((1, d), lambda i: (0, 0)),
        ],
        out_specs=[
            pl.BlockSpec((tm, d), lambda i: (jnp.minimum(i, n_sample_blocks - 1), 0)),
            pl.BlockSpec((tm, d), lambda i: (jnp.maximum(i - n_sample_blocks, 0), 0)),
        ],
        out_shape=[
            jax.ShapeDtypeStruct((rows_sample, d), F32),
            jax.ShapeDtypeStruct((rows_prompt, d), F32),
        ],
        compiler_params=pltpu.CompilerParams(dimension_semantics=("arbitrary",),
                                             vmem_limit_bytes=FFN_DOWN_VMEM_LIMIT_BYTES),
        name="ffn_down",
    )(act, wd_b, x1, norm_post.reshape(1, d))
    return yp, ys


def _rope_tables(length, start):
    pos = start + np.arange(length, dtype=np.float64)
    inv_freq = 1.0 / (ROPE_BASE ** (np.arange(0, RET_DK, 2, dtype=np.float64) / RET_DK))
    ang = pos[:, None] * inv_freq[None, :]
    return np.cos(ang), np.sin(ang)


def kernel(x_prompt, x_sample, state_ret, state_pool, norm_mix_pre, norm_mix_post, w_in,
           ret_norm_w, w_pool, pool_scale, w_out, norm_ffn_pre, norm_ffn_post,
           w_gate, w_up, w_down):
    bp, lp, d = x_prompt.shape
    bs, ls, _ = x_sample.shape
    mp, ms = bp * lp, bs * ls
    xp2 = x_prompt.reshape(mp, d)
    xs2 = x_sample.reshape(ms, d)

    qkvg, u, (w_out_b, wg_b, wu_b, wd_b) = _inproj(
        xp2, xs2, norm_mix_pre, w_in.astype(BF16), (w_out, w_gate, w_up, w_down))

    s0_prompt = jnp.zeros((bp,) + state_ret.shape[1:], F32)
    buf_prompt = jnp.zeros((bp,) + state_pool.shape[1:], F32)
    ret_p, s_p = _retention(qkvg, 0, bp, lp, s0_prompt, 0.0, ret_norm_w, n_seq=1, n_chunks=2)
    pool_p, buf_p = _pool(u, 0, bp, lp, buf_prompt, w_pool, pool_scale, 0.0, bb=1, tl=1024)
    ret_s, s_s = _retention(qkvg, mp, bs, ls, state_ret, float(PAST_LEN), ret_norm_w,
                            n_seq=8, n_chunks=1)
    pool_s, buf_s = _pool(u, mp, bs, ls, state_pool, w_pool, pool_scale, float(PAST_LEN),
                          bb=32, tl=ls)

    x1, h2 = _outproj(ret_p, ret_s, pool_p, pool_s, xp2, xs2, w_out_b,
                      norm_mix_post, norm_ffn_pre)
    act = _ffn_up(h2, wg_b, wu_b, n_col_blocks=2)
    yp, ys = _ffn_down(act, x1, mp, ms, wd_b, norm_ffn_post)

    return (yp.reshape(bp, lp, d), ys.reshape(bs, ls, d),
            s_p.astype(x_prompt.dtype), buf_p.astype(x_prompt.dtype),
            s_s.astype(state_ret.dtype), buf_s.astype(state_pool.dtype))
```

```python
import functools
import math

import jax
import jax.numpy as jnp
import numpy as np
from jax import lax
from jax.experimental import pallas as pl
from jax.experimental.pallas import tpu as pltpu

F32 = jnp.float32
BF16 = jnp.bfloat16

RET_HEADS = 4
RET_DK = 256
RET_DV = 256
RET_WIDTH = RET_HEADS * RET_DV
RET_CHUNK = 128
ROPE_BASE = 10000.0
POOL_WINDOWS = (2, 4, 8, 16)
POOL_GC = 256
POOL_WIDTH = POOL_GC * len(POOL_WINDOWS)
POOL_BUF = max(POOL_WINDOWS) - 1
POOL_HIST = POOL_BUF + 1
PAST_LEN = 16384
EPS = 1e-6

ROW_BLOCK = 512
INPROJ_ROW_BLOCK = 256
ROW_SUB_BLOCK = 128
MXU_TILE = 256
VMEM_LIMIT_BYTES = 56 * 1024 * 1024
FFN_DOWN_VMEM_LIMIT_BYTES = 62 * 1024 * 1024
INPROJ_VMEM_LIMIT_BYTES = 60 * 1024 * 1024


def _compiler_params(semantics):
    return pltpu.CompilerParams(dimension_semantics=semantics,
                                vmem_limit_bytes=VMEM_LIMIT_BYTES)


def _rms_rows(x, w):
    ms = jnp.mean(x * x, axis=-1, keepdims=True)
    return (x * lax.rsqrt(ms + EPS)) * w


def _silu(g):
    return g * (1.0 / (1.0 + jnp.exp(-g)))


def _resident(shape, index_map):
    return pl.BlockSpec(shape, index_map, pipeline_mode=pl.Buffered(1))


def _two_group_specs(block, n_prompt_blocks, n_sample_blocks, ahead=0):
    last_p = n_prompt_blocks - 1
    last_s = n_sample_blocks - 1
    return (pl.BlockSpec(block, lambda i: (jnp.minimum(i + ahead, last_p), 0)),
            pl.BlockSpec(block, lambda i: (jnp.clip(i + ahead - n_prompt_blocks, 0, last_s), 0)))


def _inproj_kernel(x0_ref, xp_ref, xs_ref, nw_ref, w_hbm, *rest, tn, n_prompt_blocks, n_cast,
                   w_rows):
    cast_src = rest[:n_cast]
    o_ref, u_ref = rest[n_cast:n_cast + 2]
    cast_dst = rest[n_cast + 2:2 * n_cast + 2]
    h_ref, w_ref, stage_ref, sem = rest[2 * n_cast + 2:]
    i = pl.program_id(0)
    slot = lax.rem(i, 2)

    def w_chunk_copy(c):
        return pltpu.make_async_copy(w_hbm.at[pl.ds(c * w_rows, w_rows), :],
                                     stage_ref.at[c % 2], sem.at[c % 2])

    @pl.when(i == 0)
    def _():
        n_w_chunks = w_hbm.shape[0] // w_rows
        w_chunk_copy(0).start()
        for c in range(n_w_chunks):
            if c + 1 < n_w_chunks:
                w_chunk_copy(c + 1).start()
            w_chunk_copy(c).wait()
            w_ref[c * w_rows:(c + 1) * w_rows, :] = stage_ref[c % 2].astype(BF16)
        h_ref[0] = _rms_rows(x0_ref[...], nw_ref[...]).astype(BF16)

    h = h_ref[slot]
    n_main = o_ref.shape[1]
    for c0 in range(0, w_ref.shape[1], tn):
        acc = jnp.dot(h, w_ref[:, c0:c0 + tn], preferred_element_type=F32)
        if c0 < n_main:
            o_ref[:, c0:c0 + tn] = acc.astype(BF16)
        else:
            u_ref[:, c0 - n_main:c0 - n_main + tn] = acc

    x_next = jnp.where(i + 1 < n_prompt_blocks, xp_ref[...], xs_ref[...])
    h_ref[1 - slot] = _rms_rows(x_next, nw_ref[...]).astype(BF16)

    for src, dst in zip(cast_src, cast_dst):
        dst[...] = src[...].astype(BF16)


def _inproj(xp2, xs2, norm_w, w_in, later_weights):
    tm = INPROJ_ROW_BLOCK
    d = xp2.shape[1]
    n = w_in.shape[1]
    w_rows = 64
    assert d % w_rows == 0
    n_main = n - POOL_WIDTH
    n_prompt_blocks = xp2.shape[0] // tm
    n_sample_blocks = xs2.shape[0] // tm
    m = xp2.shape[0] + xs2.shape[0]
    xp_spec, xs_spec = _two_group_specs((tm, d), n_prompt_blocks, n_sample_blocks, ahead=1)
    n_chunks = n_prompt_blocks
    cast_specs = [pl.BlockSpec((w.shape[0] // n_chunks, w.shape[1]),
                               lambda i: (jnp.minimum(i, n_chunks - 1), 0))
                  for w in later_weights]
    for w in later_weights:
        assert w.shape[0] % (16 * n_chunks) == 0
    outs = pl.pallas_call(
        functools.partial(_inproj_kernel, tn=2 * MXU_TILE, n_prompt_blocks=n_prompt_blocks,
                          n_cast=len(later_weights), w_rows=w_rows),
        grid=(m // tm,),
        in_specs=[
            _resident((tm, d), lambda i: (0, 0)),
            xp_spec, xs_spec,
            pl.BlockSpec((1, d), lambda i: (0, 0)),
            pl.BlockSpec(memory_space=pl.ANY),
            *cast_specs,
        ],
        out_specs=[
            pl.BlockSpec((tm, n_main), lambda i: (i, 0)),
            pl.BlockSpec((tm, POOL_WIDTH), lambda i: (i, 0)),
            *cast_specs,
        ],
        out_shape=[
            jax.ShapeDtypeStruct((m, n_main), BF16),
            jax.ShapeDtypeStruct((m, POOL_WIDTH), F32),
            *[jax.ShapeDtypeStruct(w.shape, BF16) for w in later_weights],
        ],
        scratch_shapes=[
            pltpu.VMEM((2, tm, d), BF16),
            pltpu.VMEM((d, n), BF16),
            pltpu.VMEM((2, w_rows, n), F32),
            pltpu.SemaphoreType.DMA((2,)),
        ],
        compiler_params=pltpu.CompilerParams(dimension_semantics=("arbitrary",),
                                             vmem_limit_bytes=INPROJ_VMEM_LIMIT_BYTES),
        name="inproj",
    )(xp2, xp2, xs2, norm_w.reshape(1, d), w_in, *later_weights)
    return outs[0], outs[1], outs[2:]


def _rope(x, cos, sin):
    half = x.shape[-1] // 2
    x1, x2 = x[:, :half], x[:, half:]
    return jnp.concatenate([x1 * cos - x2 * sin, x2 * cos + x1 * sin], axis=-1)


def _retention_kernel(q_ref, k_ref, v_ref, g_ref, s0_ref, cos_ref, sin_ref, cosk_ref, sink_ref,
                      decay_ref, nw_ref, o_ref, s_ref, *, n_seq, n_chunks, chunk, log_gammas,
                      single_step):
    if not single_step:
        @pl.when(pl.program_id(1) == 0)
        def _():
            s_ref[...] = s0_ref[...]

    pos = lax.broadcasted_iota(jnp.int32, (chunk, 1), 0).astype(F32)
    q_decs = [jnp.exp((pos + 1.0) * lg) for lg in log_gammas]
    k_decs = [jnp.exp((chunk - 1.0 - pos) * lg) for lg in log_gammas]
    items = [(si, h) for si in range(n_seq) for h in range(RET_HEADS)]

    def rows(si, ci):
        return slice((si * n_chunks + ci) * chunk, (si * n_chunks + ci + 1) * chunk)

    def cols(h):
        return slice(h * RET_DK, (h + 1) * RET_DK)

    for ci in range(n_chunks):
        ts = slice(ci * chunk, (ci + 1) * chunk)
        phase1 = []
        for si, h in items:
            qr = _rope(q_ref[rows(si, ci), cols(h)].astype(F32), cos_ref[ts, :], sin_ref[ts, :])
            kr = _rope(k_ref[rows(si, ci), cols(h)].astype(F32), cosk_ref[ts, :], sink_ref[ts, :])
            qb = qr.astype(BF16)
            scores = lax.dot_general(qb, kr.astype(BF16), (((1,), (1,)), ((), ())),
                                     preferred_element_type=F32) * decay_ref[h]
            phase1.append((qb, kr, scores.astype(BF16)))
        for (si, h), (qb, kr, scores_b) in zip(items, phase1):
            rs, cs = rows(si, ci), cols(h)
            vb = v_ref[rs, cs]
            s_old = s0_ref[si, h] if (single_step and ci == 0) else s_ref[si, h]
            o = jnp.dot(scores_b, vb, preferred_element_type=F32)
            o = o + jnp.dot(qb, s_old.astype(BF16), preferred_element_type=F32) * q_decs[h]
            kd = (kr * k_decs[h]).astype(BF16)
            s_ref[si, h] = math.exp(chunk * log_gammas[h]) * s_old + lax.dot_general(
                kd, vb, (((0,), (0,)), ((), ())), preferred_element_type=F32)
            o = o * lax.rsqrt(jnp.mean(o * o, axis=-1, keepdims=True) + EPS)
            o = o * nw_ref[:, cs]
            o_ref[rs, cs] = (_silu(g_ref[rs, cs].astype(F32)) * o).astype(BF16)


def _retention(qkvg, row0, b, l, s0, start, ret_norm_w, *, n_seq, n_chunks):
    chunk = RET_CHUNK if l % RET_CHUNK == 0 else l
    log_gammas = tuple(math.log(1.0 - 2.0 ** (-5.0 - h)) for h in range(RET_HEADS))
    half = RET_DK // 2
    steps_per_seq = l // (chunk * n_chunks)
    block_rows = n_seq * n_chunks * chunk
    assert row0 % block_rows == 0 and (n_seq == 1 or steps_per_seq == 1)
    block0 = row0 // block_rows

    cos, sin = _rope_tables(l, start)
    k_scale = RET_DK ** -0.5
    idx = np.arange(chunk, dtype=np.float64)
    diff = idx[:, None] - idx[None, :]
    lg = np.asarray(log_gammas, np.float64)[:, None, None]
    decay = np.where(diff[None] >= 0, np.exp(np.maximum(diff[None], 0.0) * lg), 0.0)
    tables = [jnp.asarray(t, F32) for t in (cos, sin, cos * k_scale, sin * k_scale, decay)]

    def col_block(idx):
        return pl.BlockSpec((block_rows, RET_WIDTH),
                            lambda i, c: (block0 + i * steps_per_seq + c, idx))

    state_spec = pl.BlockSpec((n_seq, RET_HEADS, RET_DK, RET_DV), lambda i, c: (i, 0, 0, 0))
    table_spec = pl.BlockSpec((n_chunks * chunk, half), lambda i, c: (c, 0))
    return pl.pallas_call(
        functools.partial(_retention_kernel, n_seq=n_seq, n_chunks=n_chunks, chunk=chunk,
                          log_gammas=log_gammas, single_step=steps_per_seq == 1),
        grid=(b // n_seq, steps_per_seq),
        in_specs=[
            col_block(0), col_block(1), col_block(2), col_block(3),
            state_spec,
            table_spec, table_spec, table_spec, table_spec,
            pl.BlockSpec((RET_HEADS, chunk, chunk), lambda i, c: (0, 0, 0)),
            pl.BlockSpec((1, RET_WIDTH), lambda i, c: (0, 0)),
        ],
        out_specs=[
            pl.BlockSpec((block_rows, RET_WIDTH), lambda i, c: (i * steps_per_seq + c, 0)),
            state_spec,
        ],
        out_shape=[
            jax.ShapeDtypeStruct((b * l, RET_WIDTH), BF16),
            jax.ShapeDtypeStruct((b, RET_HEADS, RET_DK, RET_DV), F32),
        ],
        compiler_params=_compiler_params(("parallel", "arbitrary")),
        name="retention",
    )(qkvg, qkvg, qkvg, qkvg, s0, *tables, ret_norm_w.reshape(1, RET_WIDTH))


def _pool_kernel(u_ref, buf_ref, wp_ref, ps_ref, o_ref, nb_ref, xp_ref, *, bb, tl, start):
    t = pl.program_id(1)

    @pl.when(t == 0)
    def _():
        xp_ref[:, 0:1, :] = jnp.zeros((bb, 1, POOL_WIDTH), F32)
        xp_ref[:, 1:POOL_HIST, :] = buf_ref[...]

    xp_ref[:, POOL_HIST:POOL_HIST + tl, :] = u_ref[...].reshape(bb, tl, POOL_WIDTH)
    pos = (lax.broadcasted_iota(jnp.int32, (bb, tl, POOL_GC), 1) + t * tl).astype(F32) + start
    rows = POOL_HIST + tl

    for g, w in enumerate(POOL_WINDOWS):
        cs = slice(g * POOL_GC, (g + 1) * POOL_GC)
        u = xp_ref[:, POOL_HIST:POOL_HIST + tl, cs]
        acc = xp_ref[:, :, cs].reshape(bb * rows, POOL_GC)
        shift = 1
        while shift < w:
            acc = acc + pltpu.roll(acc, shift, 0)
            shift *= 2
        acc = acc.reshape(bb, rows, POOL_GC)[:, POOL_HIST:, :]
        cnt = jnp.minimum(pos + 1.0, float(w))
        pooled = (acc / cnt - u).astype(BF16).reshape(bb * tl, POOL_GC)
        y = jnp.dot(pooled, wp_ref[g].astype(BF16), preferred_element_type=F32)
        o_ref[:, cs] = (y * ps_ref[:, cs]).astype(BF16)

    nb_ref[...] = xp_ref[:, tl + 1:tl + POOL_HIST, :]
    if tl >= POOL_HIST:
        xp_ref[:, 0:POOL_HIST, :] = xp_ref[:, tl:tl + POOL_HIST, :]


def _pool(u_all, row0, b, l, buf, w_pool, pool_scale, start, *, bb, tl):
    assert l == tl or (bb == 1 and tl >= POOL_HIST)
    steps_per_seq = l // tl
    block_rows = bb * tl
    assert row0 % block_rows == 0
    block0 = row0 // block_rows
    return pl.pallas_call(
        functools.partial(_pool_kernel, bb=bb, tl=tl, start=float(start)),
        grid=(b // bb, steps_per_seq),
        in_specs=[
            pl.BlockSpec((block_rows, POOL_WIDTH),
                         lambda i, t: (block0 + i * steps_per_seq + t, 0)),
            pl.BlockSpec((bb, POOL_BUF, POOL_WIDTH), lambda i, t: (i, 0, 0)),
            pl.BlockSpec(w_pool.shape, lambda i, t: (0, 0, 0)),
            pl.BlockSpec((1, POOL_WIDTH), lambda i, t: (0, 0)),
        ],
        out_specs=[
            pl.BlockSpec((block_rows, POOL_WIDTH), lambda i, t: (i * steps_per_seq + t, 0)),
            pl.BlockSpec((bb, POOL_BUF, POOL_WIDTH), lambda i, t: (i, 0, 0)),
        ],
        out_shape=[
            jax.ShapeDtypeStruct((b * l, POOL_WIDTH), BF16),
            jax.ShapeDtypeStruct((b, POOL_BUF, POOL_WIDTH), F32),
        ],
        scratch_shapes=[pltpu.VMEM((bb, POOL_HIST + tl, POOL_WIDTH), F32)],
        compiler_params=_compiler_params(("parallel", "arbitrary")),
        name="pool",
    )(u_all, buf, w_pool, pool_scale.reshape(1, POOL_WIDTH))


def _outproj_kernel(rp_ref, rs_ref, pp_ref, ps_ref, xp_ref, xs_ref, w_ref, nw1_ref, nw2_ref,
                    x1_ref, h2_ref, *, tm, sub, n_prompt_blocks):
    is_prompt = pl.program_id(0) < n_prompt_blocks
    for r0 in range(0, tm, sub):
        rs = slice(r0, r0 + sub)
        ret = jnp.where(is_prompt, rp_ref[rs, :], rs_ref[rs, :])
        pool = jnp.where(is_prompt, pp_ref[rs, :], ps_ref[rs, :])
        mix = jnp.dot(ret, w_ref[0:RET_WIDTH, :], preferred_element_type=F32)
        mix = mix + jnp.dot(pool, w_ref[RET_WIDTH:, :], preferred_element_type=F32)
        x1 = jnp.where(is_prompt, xp_ref[rs, :], xs_ref[rs, :]) + _rms_rows(mix, nw1_ref[...])
        x1_ref[rs, :] = x1
        h2_ref[rs, :] = _rms_rows(x1, nw2_ref[...]).astype(BF16)


def _outproj(ret_p, ret_s, pool_p, pool_s, xp2, xs2, w_out_b, norm_post, norm_ffn_pre):
    tm = ROW_BLOCK
    d = xp2.shape[1]
    n_prompt_blocks = xp2.shape[0] // tm
    n_sample_blocks = xs2.shape[0] // tm
    m = xp2.shape[0] + xs2.shape[0]
    rp_spec, rs_spec = _two_group_specs((tm, RET_WIDTH), n_prompt_blocks, n_sample_blocks)
    pp_spec, ps_spec = _two_group_specs((tm, POOL_WIDTH), n_prompt_blocks, n_sample_blocks)
    xp_spec, xs_spec = _two_group_specs((tm, d), n_prompt_blocks, n_sample_blocks)
    return pl.pallas_call(
        functools.partial(_outproj_kernel, tm=tm, sub=ROW_SUB_BLOCK,
                          n_prompt_blocks=n_prompt_blocks),
        grid=(m // tm,),
        in_specs=[
            rp_spec, rs_spec, pp_spec, ps_spec, xp_spec, xs_spec,
            _resident(w_out_b.shape, lambda i: (0, 0)),
            pl.BlockSpec((1, d), lambda i: (0, 0)),
            pl.BlockSpec((1, d), lambda i: (0, 0)),
        ],
        out_specs=[
            pl.BlockSpec((tm, d), lambda i: (i, 0)),
            pl.BlockSpec((tm, d), lambda i: (i, 0)),
        ],
        out_shape=[
            jax.ShapeDtypeStruct((m, d), F32),
            jax.ShapeDtypeStruct((m, d), BF16),
        ],
        compiler_params=_compiler_params(("arbitrary",)),
        name="outproj",
    )(ret_p, ret_s, pool_p, pool_s, xp2, xs2, w_out_b,
      norm_post.reshape(1, d), norm_ffn_pre.reshape(1, d))


def _ffn_up_kernel(h_ref, wg_ref, wu_ref, a_ref, *, sub):
    h = h_ref[...]
    for c0 in range(0, a_ref.shape[1], sub):
        cs = slice(c0, c0 + sub)
        gate = jnp.dot(h, wg_ref[:, cs], preferred_element_type=F32)
        up = jnp.dot(h, wu_ref[:, cs], preferred_element_type=F32)
        a_ref[:, cs] = (_silu(gate) * up).astype(BF16)


def _ffn_up(h2, wg_b, wu_b, *, n_col_blocks):
    tm = 2 * ROW_BLOCK
    m, d = h2.shape
    assert m % tm == 0
    dff = wg_b.shape[1]
    tf = dff // n_col_blocks
    assert tf % MXU_TILE == 0
    return pl.pallas_call(
        functools.partial(_ffn_up_kernel, sub=MXU_TILE),
        grid=(n_col_blocks, m // tm),
        in_specs=[
            pl.BlockSpec((tm, d), lambda j, i: (i, 0)),
            _resident((d, tf), lambda j, i: (0, j)),
            _resident((d, tf), lambda j, i: (0, j)),
        ],
        out_specs=pl.BlockSpec((tm, tf), lambda j, i: (i, j)),
        out_shape=jax.ShapeDtypeStruct((m, dff), BF16),
        compiler_params=_compiler_params(("arbitrary", "arbitrary")),
        name="ffn_up",
    )(h2, wg_b, wu_b)


def _ffn_down_kernel(a_ref, wd_ref, x1_ref, nw_ref, ys_ref, yp_ref, *, tm, sub,
                     n_sample_blocks):
    i = pl.program_id(0)
    is_sample = i < n_sample_blocks

    @pl.when(i == 0)
    def _():
        yp_ref[...] = jnp.zeros(yp_ref.shape, F32)

    for r0 in range(0, tm, sub):
        rs = slice(r0, r0 + sub)
        ff = jnp.dot(a_ref[rs, :], wd_ref[...], preferred_element_type=F32)
        y = x1_ref[rs, :] + _rms_rows(ff, nw_ref[...])
        ys_ref[rs, :] = jnp.where(is_sample, y, ys_ref[rs, :])
        yp_ref[rs, :] = jnp.where(is_sample, yp_ref[rs, :], y)


def _ffn_down(act, x1, rows_prompt, rows_sample, wd_b, norm_post):
    tm = ROW_BLOCK
    dff, d = wd_b.shape
    n_prompt_blocks = rows_prompt // tm
    n_sample_blocks = rows_sample // tm

    def in_block(i):
        return jnp.where(i < n_sample_blocks, n_prompt_blocks + i, i - n_sample_blocks)

    ys, yp = pl.pallas_call(
        functools.partial(_ffn_down_kernel, tm=tm, sub=MXU_TILE,
                          n_sample_blocks=n_sample_blocks),
        grid=(n_prompt_blocks + n_sample_blocks,),
        in_specs=[
            pl.BlockSpec((tm, dff), lambda i: (in_block(i), 0)),
            _resident((dff, d), lambda i: (0, 0)),
            pl.BlockSpec((tm, d), lambda i: (in_block(i), 0)),
            pl.BlockSpec((1, d), lambda i: (0, 0)),
        ],
        out_specs=[
            pl.BlockSpec((tm, d), lambda i: (jnp.minimum(i, n_sample_blocks - 1), 0)),
            pl.BlockSpec((tm, d), lambda i: (jnp.maximum(i - n_sample_blocks, 0), 0)),
        ],
        out_shape=[
            jax.ShapeDtypeStruct((rows_sample, d), F32),
            jax.ShapeDtypeStruct((rows_prompt, d), F32),
        ],
        compiler_params=pltpu.CompilerParams(dimension_semantics=("arbitrary",),
                                             vmem_limit_bytes=FFN_DOWN_VMEM_LIMIT_BYTES),
        name="ffn_down",
    )(act, wd_b, x1, norm_post.reshape(1, d))
    return yp, ys


def _rope_tables(length, start):
    pos = start + np.arange(length, dtype=np.float64)
    inv_freq = 1.0 / (ROPE_BASE ** (np.arange(0, RET_DK, 2, dtype=np.float64) / RET_DK))
    ang = pos[:, None] * inv_freq[None, :]
    return np.cos(ang), np.sin(ang)


def kernel(x_prompt, x_sample, state_ret, state_pool, norm_mix_pre, norm_mix_post, w_in,
           ret_norm_w, w_pool, pool_scale, w_out, norm_ffn_pre, norm_ffn_post,
           w_gate, w_up, w_down):
    bp, lp, d = x_prompt.shape
    bs, ls, _ = x_sample.shape
    mp, ms = bp * lp, bs * ls
    xp2 = x_prompt.reshape(mp, d)
    xs2 = x_sample.reshape(ms, d)

    qkvg, u, (w_out_b, wg_b, wu_b, wd_b) = _inproj(
        xp2, xs2, norm_mix_pre, w_in, (w_out, w_gate, w_up, w_down))

    s0_prompt = jnp.zeros((bp,) + state_ret.shape[1:], F32)
    buf_prompt = jnp.zeros((bp,) + state_pool.shape[1:], F32)
    ret_p, s_p = _retention(qkvg, 0, bp, lp, s0_prompt, 0.0, ret_norm_w, n_seq=1, n_chunks=2)
    pool_p, buf_p = _pool(u, 0, bp, lp, buf_prompt, w_pool, pool_scale, 0.0, bb=1, tl=1024)
    ret_s, s_s = _retention(qkvg, mp, bs, ls, state_ret, float(PAST_LEN), ret_norm_w,
                            n_seq=8, n_chunks=1)
    pool_s, buf_s = _pool(u, mp, bs, ls, state_pool, w_pool, pool_scale, float(PAST_LEN),
                          bb=32, tl=ls)

    x1, h2 = _outproj(ret_p, ret_s, pool_p, pool_s, xp2, xs2, w_out_b,
                      norm_mix_post, norm_ffn_pre)
    act = _ffn_up(h2, wg_b, wu_b, n_col_blocks=2)
    yp, ys = _ffn_down(act, x1, mp, ms, wd_b, norm_ffn_post)

    return (yp.reshape(bp, lp, d), ys.reshape(bs, ls, d),
            s_p.astype(x_prompt.dtype), buf_p.astype(x_prompt.dtype),
            s_s.astype(state_ret.dtype), buf_s.astype(state_pool.dtype))
```

```python
import functools
import math

import jax
import jax.numpy as jnp
import numpy as np
from jax import lax
from jax.experimental import pallas as pl
from jax.experimental.pallas import tpu as pltpu

F32 = jnp.float32
BF16 = jnp.bfloat16

RET_HEADS = 4
RET_DK = 256
RET_DV = 256
RET_WIDTH = RET_HEADS * RET_DV
RET_CHUNK = 128
ROPE_BASE = 10000.0
POOL_WINDOWS = (2, 4, 8, 16)
POOL_GC = 256
POOL_WIDTH = POOL_GC * len(POOL_WINDOWS)
POOL_BUF = max(POOL_WINDOWS) - 1
POOL_HIST = POOL_BUF + 1
PAST_LEN = 16384
EPS = 1e-6

ROW_BLOCK = 512
INPROJ_ROW_BLOCK = 256
ROW_SUB_BLOCK = 128
MXU_TILE = 256
VMEM_LIMIT_BYTES = 56 * 1024 * 1024
FFN_DOWN_VMEM_LIMIT_BYTES = 62 * 1024 * 1024
INPROJ_VMEM_LIMIT_BYTES = 62 * 1024 * 1024
W_STAGE_SLOTS = 4


def _compiler_params(semantics):
    return pltpu.CompilerParams(dimension_semantics=semantics,
                                vmem_limit_bytes=VMEM_LIMIT_BYTES)


def _rms_rows(x, w):
    ms = jnp.mean(x * x, axis=-1, keepdims=True)
    return (x * lax.rsqrt(ms + EPS)) * w


def _silu(g):
    return g * (1.0 / (1.0 + jnp.exp(-g)))


def _resident(shape, index_map):
    return pl.BlockSpec(shape, index_map, pipeline_mode=pl.Buffered(1))


def _two_group_specs(block, n_prompt_blocks, n_sample_blocks, ahead=0):
    last_p = n_prompt_blocks - 1
    last_s = n_sample_blocks - 1
    return (pl.BlockSpec(block, lambda i: (jnp.minimum(i + ahead, last_p), 0)),
            pl.BlockSpec(block, lambda i: (jnp.clip(i + ahead - n_prompt_blocks, 0, last_s), 0)))


def _inproj_kernel(x0_ref, xp_ref, xs_ref, nw_ref, w_hbm, *rest, tn, n_prompt_blocks, n_cast,
                   w_rows):
    cast_src = rest[:n_cast]
    o_ref, u_ref = rest[n_cast:n_cast + 2]
    cast_dst = rest[n_cast + 2:2 * n_cast + 2]
    h_ref, w_ref, stage_ref, sem = rest[2 * n_cast + 2:]
    i = pl.program_id(0)
    slot = lax.rem(i, 2)

    n_slots = stage_ref.shape[0]

    def w_chunk_copy(c):
        return pltpu.make_async_copy(w_hbm.at[pl.ds(c * w_rows, w_rows), :],
                                     stage_ref.at[c % n_slots], sem.at[c % n_slots])

    @pl.when(i == 0)
    def _():
        n_w_chunks = w_hbm.shape[0] // w_rows
        for c in range(n_slots - 1):
            w_chunk_copy(c).start()
        for c in range(n_w_chunks):
            if c + n_slots - 1 < n_w_chunks:
                w_chunk_copy(c + n_slots - 1).start()
            w_chunk_copy(c).wait()
            w_ref[c * w_rows:(c + 1) * w_rows, :] = stage_ref[c % n_slots].astype(BF16)
        h_ref[0] = _rms_rows(x0_ref[...], nw_ref[...]).astype(BF16)

    h = h_ref[slot]
    n_main = o_ref.shape[1]
    for c0 in range(0, w_ref.shape[1], tn):
        acc = jnp.dot(h, w_ref[:, c0:c0 + tn], preferred_element_type=F32)
        if c0 < n_main:
            o_ref[:, c0:c0 + tn] = acc.astype(BF16)
        else:
            u_ref[:, c0 - n_main:c0 - n_main + tn] = acc

    x_next = jnp.where(i + 1 < n_prompt_blocks, xp_ref[...], xs_ref[...])
    h_ref[1 - slot] = _rms_rows(x_next, nw_ref[...]).astype(BF16)

    for src, dst in zip(cast_src, cast_dst):
        dst[...] = src[...].astype(BF16)


def _inproj(xp2, xs2, norm_w, w_in, later_weights):
    tm = INPROJ_ROW_BLOCK
    d = xp2.shape[1]
    n = w_in.shape[1]
    w_rows = 64
    assert d % w_rows == 0
    n_main = n - POOL_WIDTH
    n_prompt_blocks = xp2.shape[0] // tm
    n_sample_blocks = xs2.shape[0] // tm
    m = xp2.shape[0] + xs2.shape[0]
    xp_spec, xs_spec = _two_group_specs((tm, d), n_prompt_blocks, n_sample_blocks, ahead=1)
    n_chunks = n_prompt_blocks
    cast_specs = [pl.BlockSpec((w.shape[0] // n_chunks, w.shape[1]),
                               lambda i: (jnp.minimum(i, n_chunks - 1), 0))
                  for w in later_weights]
    for w in later_weights:
        assert w.shape[0] % (16 * n_chunks) == 0
    outs = pl.pallas_call(
        functools.partial(_inproj_kernel, tn=2 * MXU_TILE, n_prompt_blocks=n_prompt_blocks,
                          n_cast=len(later_weights), w_rows=w_rows),
        grid=(m // tm,),
        in_specs=[
            _resident((tm, d), lambda i: (0, 0)),
            xp_spec, xs_spec,
            pl.BlockSpec((1, d), lambda i: (0, 0)),
            pl.BlockSpec(memory_space=pl.ANY),
            *cast_specs,
        ],
        out_specs=[
            pl.BlockSpec((tm, n_main), lambda i: (i, 0)),
            pl.BlockSpec((tm, POOL_WIDTH), lambda i: (i, 0)),
            *cast_specs,
        ],
        out_shape=[
            jax.ShapeDtypeStruct((m, n_main), BF16),
            jax.ShapeDtypeStruct((m, POOL_WIDTH), F32),
            *[jax.ShapeDtypeStruct(w.shape, BF16) for w in later_weights],
        ],
        scratch_shapes=[
            pltpu.VMEM((2, tm, d), BF16),
            pltpu.VMEM((d, n), BF16),
            pltpu.VMEM((W_STAGE_SLOTS, w_rows, n), F32),
            pltpu.SemaphoreType.DMA((W_STAGE_SLOTS,)),
        ],
        compiler_params=pltpu.CompilerParams(dimension_semantics=("arbitrary",),
                                             vmem_limit_bytes=INPROJ_VMEM_LIMIT_BYTES),
        name="inproj",
    )(xp2, xp2, xs2, norm_w.reshape(1, d), w_in, *later_weights)
    return outs[0], outs[1], outs[2:]


def _rope(x, cos, sin):
    half = x.shape[-1] // 2
    x1, x2 = x[:, :half], x[:, half:]
    return jnp.concatenate([x1 * cos - x2 * sin, x2 * cos + x1 * sin], axis=-1)


def _retention_kernel(q_ref, k_ref, v_ref, g_ref, s0_ref, cos_ref, sin_ref, cosk_ref, sink_ref,
                      decay_ref, nw_ref, o_ref, s_ref, *, n_seq, n_chunks, chunk, log_gammas,
                      single_step):
    if not single_step:
        @pl.when(pl.program_id(1) == 0)
        def _():
            s_ref[...] = s0_ref[...]

    pos = lax.broadcasted_iota(jnp.int32, (chunk, 1), 0).astype(F32)
    q_decs = [jnp.exp((pos + 1.0) * lg) for lg in log_gammas]
    k_decs = [jnp.exp((chunk - 1.0 - pos) * lg) for lg in log_gammas]
    items = [(si, h) for si in range(n_seq) for h in range(RET_HEADS)]

    def rows(si, ci):
        return slice((si * n_chunks + ci) * chunk, (si * n_chunks + ci + 1) * chunk)

    def cols(h):
        return slice(h * RET_DK, (h + 1) * RET_DK)

    for ci in range(n_chunks):
        ts = slice(ci * chunk, (ci + 1) * chunk)
        phase1 = []
        for si, h in items:
            qr = _rope(q_ref[rows(si, ci), cols(h)].astype(F32), cos_ref[ts, :], sin_ref[ts, :])
            kr = _rope(k_ref[rows(si, ci), cols(h)].astype(F32), cosk_ref[ts, :], sink_ref[ts, :])
            qb = qr.astype(BF16)
            scores = lax.dot_general(qb, kr.astype(BF16), (((1,), (1,)), ((), ())),
                                     preferred_element_type=F32) * decay_ref[h]
            phase1.append((qb, kr, scores.astype(BF16)))
        for (si, h), (qb, kr, scores_b) in zip(items, phase1):
            rs, cs = rows(si, ci), cols(h)
            vb = v_ref[rs, cs]
            s_old = s0_ref[si, h] if (single_step and ci == 0) else s_ref[si, h]
            o = jnp.dot(scores_b, vb, preferred_element_type=F32)
            o = o + jnp.dot(qb, s_old.astype(BF16), preferred_element_type=F32) * q_decs[h]
            kd = (kr * k_decs[h]).astype(BF16)
            s_ref[si, h] = math.exp(chunk * log_gammas[h]) * s_old + lax.dot_general(
                kd, vb, (((0,), (0,)), ((), ())), preferred_element_type=F32)
            o = o * lax.rsqrt(jnp.mean(o * o, axis=-1, keepdims=True) + EPS)
            o = o * nw_ref[:, cs]
            o_ref[rs, cs] = (_silu(g_ref[rs, cs].astype(F32)) * o).astype(BF16)


def _retention(qkvg, row0, b, l, s0, start, ret_norm_w, *, n_seq, n_chunks):
    chunk = RET_CHUNK if l % RET_CHUNK == 0 else l
    log_gammas = tuple(math.log(1.0 - 2.0 ** (-5.0 - h)) for h in range(RET_HEADS))
    half = RET_DK // 2
    steps_per_seq = l // (chunk * n_chunks)
    block_rows = n_seq * n_chunks * chunk
    assert row0 % block_rows == 0 and (n_seq == 1 or steps_per_seq == 1)
    block0 = row0 // block_rows

    cos, sin = _rope_tables(l, start)
    k_scale = RET_DK ** -0.5
    idx = np.arange(chunk, dtype=np.float64)
    diff = idx[:, None] - idx[None, :]
    lg = np.asarray(log_gammas, np.float64)[:, None, None]
    decay = np.where(diff[None] >= 0, np.exp(np.maximum(diff[None], 0.0) * lg), 0.0)
    tables = [jnp.asarray(t, F32) for t in (cos, sin, cos * k_scale, sin * k_scale, decay)]

    def col_block(idx):
        return pl.BlockSpec((block_rows, RET_WIDTH),
                            lambda i, c: (block0 + i * steps_per_seq + c, idx))

    state_spec = pl.BlockSpec((n_seq, RET_HEADS, RET_DK, RET_DV), lambda i, c: (i, 0, 0, 0))
    table_spec = pl.BlockSpec((n_chunks * chunk, half), lambda i, c: (c, 0))
    return pl.pallas_call(
        functools.partial(_retention_kernel, n_seq=n_seq, n_chunks=n_chunks, chunk=chunk,
                          log_gammas=log_gammas, single_step=steps_per_seq == 1),
        grid=(b // n_seq, steps_per_seq),
        in_specs=[
            col_block(0), col_block(1), col_block(2), col_block(3),
            state_spec,
            table_spec, table_spec, table_spec, table_spec,
            pl.BlockSpec((RET_HEADS, chunk, chunk), lambda i, c: (0, 0, 0)),
            pl.BlockSpec((1, RET_WIDTH), lambda i, c: (0, 0)),
        ],
        out_specs=[
            pl.BlockSpec((block_rows, RET_WIDTH), lambda i, c: (i * steps_per_seq + c, 0)),
            state_spec,
        ],
        out_shape=[
            jax.ShapeDtypeStruct((b * l, RET_WIDTH), BF16),
            jax.ShapeDtypeStruct((b, RET_HEADS, RET_DK, RET_DV), F32),
        ],
        compiler_params=_compiler_params(("parallel", "arbitrary")),
        name="retention",
    )(qkvg, qkvg, qkvg, qkvg, s0, *tables, ret_norm_w.reshape(1, RET_WIDTH))


def _pool_kernel(u_ref, buf_ref, wp_ref, ps_ref, o_ref, nb_ref, xp_ref, *, bb, tl, start):
    t = pl.program_id(1)

    @pl.when(t == 0)
    def _():
        xp_ref[:, 0:1, :] = jnp.zeros((bb, 1, POOL_WIDTH), F32)
        xp_ref[:, 1:POOL_HIST, :] = buf_ref[...]

    xp_ref[:, POOL_HIST:POOL_HIST + tl, :] = u_ref[...].reshape(bb, tl, POOL_WIDTH)
    pos = (lax.broadcasted_iota(jnp.int32, (bb, tl, POOL_GC), 1) + t * tl).astype(F32) + start
    rows = POOL_HIST + tl

    for g, w in enumerate(POOL_WINDOWS):
        cs = slice(g * POOL_GC, (g + 1) * POOL_GC)
        u = xp_ref[:, POOL_HIST:POOL_HIST + tl, cs]
        acc = xp_ref[:, :, cs].reshape(bb * rows, POOL_GC)
        shift = 1
        while shift < w:
            acc = acc + pltpu.roll(acc, shift, 0)
            shift *= 2
        acc = acc.reshape(bb, rows, POOL_GC)[:, POOL_HIST:, :]
        cnt = jnp.minimum(pos + 1.0, float(w))
        pooled = (acc / cnt - u).astype(BF16).reshape(bb * tl, POOL_GC)
        y = jnp.dot(pooled, wp_ref[g].astype(BF16), preferred_element_type=F32)
        o_ref[:, cs] = (y * ps_ref[:, cs]).astype(BF16)

    nb_ref[...] = xp_ref[:, tl + 1:tl + POOL_HIST, :]
    if tl >= POOL_HIST:
        xp_ref[:, 0:POOL_HIST, :] = xp_ref[:, tl:tl + POOL_HIST, :]


def _pool(u_all, row0, b, l, buf, w_pool, pool_scale, start, *, bb, tl):
    assert l == tl or (bb == 1 and tl >= POOL_HIST)
    steps_per_seq = l // tl
    block_rows = bb * tl
    assert row0 % block_rows == 0
    block0 = row0 // block_rows
    return pl.pallas_call(
        functools.partial(_pool_kernel, bb=bb, tl=tl, start=float(start)),
        grid=(b // bb, steps_per_seq),
        in_specs=[
            pl.BlockSpec((block_rows, POOL_WIDTH),
                         lambda i, t: (block0 + i * steps_per_seq + t, 0)),
            pl.BlockSpec((bb, POOL_BUF, POOL_WIDTH), lambda i, t: (i, 0, 0)),
            pl.BlockSpec(w_pool.shape, lambda i, t: (0, 0, 0)),
            pl.BlockSpec((1, POOL_WIDTH), lambda i, t: (0, 0)),
        ],
        out_specs=[
            pl.BlockSpec((block_rows, POOL_WIDTH), lambda i, t: (i * steps_per_seq + t, 0)),
            pl.BlockSpec((bb, POOL_BUF, POOL_WIDTH), lambda i, t: (i, 0, 0)),
        ],
        out_shape=[
            jax.ShapeDtypeStruct((b * l, POOL_WIDTH), BF16),
            jax.ShapeDtypeStruct((b, POOL_BUF, POOL_WIDTH), F32),
        ],
        scratch_shapes=[pltpu.VMEM((bb, POOL_HIST + tl, POOL_WIDTH), F32)],
        compiler_params=_compiler_params(("parallel", "arbitrary")),
        name="pool",
    )(u_all, buf, w_pool, pool_scale.reshape(1, POOL_WIDTH))


def _outproj_kernel(rp_ref, rs_ref, pp_ref, ps_ref, xp_ref, xs_ref, w_ref, nw1_ref, nw2_ref,
                    x1_ref, h2_ref, *, tm, sub, n_prompt_blocks):
    is_prompt = pl.program_id(0) < n_prompt_blocks
    for r0 in range(0, tm, sub):
        rs = slice(r0, r0 + sub)
        ret = jnp.where(is_prompt, rp_ref[rs, :], rs_ref[rs, :])
        pool = jnp.where(is_prompt, pp_ref[rs, :], ps_ref[rs, :])
        mix = jnp.dot(ret, w_ref[0:RET_WIDTH, :], preferred_element_type=F32)
        mix = mix + jnp.dot(pool, w_ref[RET_WIDTH:, :], preferred_element_type=F32)
        x1 = jnp.where(is_prompt, xp_ref[rs, :], xs_ref[rs, :]) + _rms_rows(mix, nw1_ref[...])
        x1_ref[rs, :] = x1
        h2_ref[rs, :] = _rms_rows(x1, nw2_ref[...]).astype(BF16)


def _outproj(ret_p, ret_s, pool_p, pool_s, xp2, xs2, w_out_b, norm_post, norm_ffn_pre):
    tm = ROW_BLOCK
    d = xp2.shape[1]
    n_prompt_blocks = xp2.shape[0] // tm
    n_sample_blocks = xs2.shape[0] // tm
    m = xp2.shape[0] + xs2.shape[0]
    rp_spec, rs_spec = _two_group_specs((tm, RET_WIDTH), n_prompt_blocks, n_sample_blocks)
    pp_spec, ps_spec = _two_group_specs((tm, POOL_WIDTH), n_prompt_blocks, n_sample_blocks)
    xp_spec, xs_spec = _two_group_specs((tm, d), n_prompt_blocks, n_sample_blocks)
    return pl.pallas_call(
        functools.partial(_outproj_kernel, tm=tm, sub=ROW_SUB_BLOCK,
                          n_prompt_blocks=n_prompt_blocks),
        grid=(m // tm,),
        in_specs=[
            rp_spec, rs_spec, pp_spec, ps_spec, xp_spec, xs_spec,
            _resident(w_out_b.shape, lambda i: (0, 0)),
            pl.BlockSpec((1, d), lambda i: (0, 0)),
            pl.BlockSpec((1, d), lambda i: (0, 0)),
        ],
        out_specs=[
            pl.BlockSpec((tm, d), lambda i: (i, 0)),
            pl.BlockSpec((tm, d), lambda i: (i, 0)),
        ],
        out_shape=[
            jax.ShapeDtypeStruct((m, d), F32),
            jax.ShapeDtypeStruct((m, d), BF16),
        ],
        compiler_params=_compiler_params(("arbitrary",)),
        name="outproj",
    )(ret_p, ret_s, pool_p, pool_s, xp2, xs2, w_out_b,
      norm_post.reshape(1, d), norm_ffn_pre.reshape(1, d))


def _ffn_up_kernel(h_ref, wg_ref, wu_ref, a_ref, *, sub):
    h = h_ref[...]
    for c0 in range(0, a_ref.shape[1], sub):
        cs = slice(c0, c0 + sub)
        gate = jnp.dot(h, wg_ref[:, cs], preferred_element_type=F32)
        up = jnp.dot(h, wu_ref[:, cs], preferred_element_type=F32)
        a_ref[:, cs] = (_silu(gate) * up).astype(BF16)


def _ffn_up(h2, wg_b, wu_b, *, n_col_blocks):
    tm = 2 * ROW_BLOCK
    m, d = h2.shape
    assert m % tm == 0
    dff = wg_b.shape[1]
    tf = dff // n_col_blocks
    assert tf % MXU_TILE == 0
    return pl.pallas_call(
        functools.partial(_ffn_up_kernel, sub=MXU_TILE),
        grid=(n_col_blocks, m // tm),
        in_specs=[
            pl.BlockSpec((tm, d), lambda j, i: (i, 0)),
            _resident((d, tf), lambda j, i: (0, j)),
            _resident((d, tf), lambda j, i: (0, j)),
        ],
        out_specs=pl.BlockSpec((tm, tf), lambda j, i: (i, j)),
        out_shape=jax.ShapeDtypeStruct((m, dff), BF16),
        compiler_params=_compiler_params(("arbitrary", "arbitrary")),
        name="ffn_up",
    )(h2, wg_b, wu_b)


def _ffn_down_kernel(a_ref, wd_ref, x1_ref, nw_ref, ys_ref, yp_ref, *, tm, sub,
                     n_sample_blocks):
    is_sample = pl.program_id(0) < n_sample_blocks
    for r0 in range(0, tm, sub):
        rs = slice(r0, r0 + sub)
        ff = jnp.dot(a_ref[rs, :], wd_ref[...], preferred_element_type=F32)
        y = x1_ref[rs, :] + _rms_rows(ff, nw_ref[...])
        yp_ref[rs, :] = y
        ys_ref[rs, :] = jnp.where(is_sample, y, ys_ref[rs, :])


def _ffn_down(act, x1, rows_prompt, rows_sample, wd_b, norm_post):
    tm = ROW_BLOCK
    dff, d = wd_b.shape
    n_prompt_blocks = rows_prompt // tm
    n_sample_blocks = rows_sample // tm

    def in_block(i):
        return jnp.where(i < n_sample_blocks, n_prompt_blocks + i, i - n_sample_blocks)

    ys, yp = pl.pallas_call(
        functools.partial(_ffn_down_kernel, tm=tm, sub=MXU_TILE,
                          n_sample_blocks=n_sample_blocks),
        grid=(n_prompt_blocks + n_sample_blocks,),
        in_specs=[
            pl.BlockSpec((tm, dff), lambda i: (in_block(i), 0)),
            _resident((dff, d), lambda i: (0, 0)),
            pl.BlockSpec((tm, d), lambda i: (in_block(i), 0)),
            pl.BlockSpec((1, d), lambda i: (0, 0)),
        ],
        out_specs=[
            pl.BlockSpec((tm, d), lambda i: (jnp.minimum(i, n_sample_blocks - 1), 0)),
            pl.BlockSpec((tm, d), lambda i: (jnp.maximum(i - n_sample_blocks, 0), 0)),
        ],
        out_shape=[
            jax.ShapeDtypeStruct((rows_sample, d), F32),
            jax.ShapeDtypeStruct((rows_prompt, d), F32),
        ],
        compiler_params=pltpu.CompilerParams(dimension_semantics=("arbitrary",),
                                             vmem_limit_bytes=FFN_DOWN_VMEM_LIMIT_BYTES),
        name="ffn_down",
    )(act, wd_b, x1, norm_post.reshape(1, d))
    return yp, ys


def _rope_tables(length, start):
    pos = start + np.arange(length, dtype=np.float64)
    inv_freq = 1.0 / (ROPE_BASE ** (np.arange(0, RET_DK, 2, dtype=np.float64) / RET_DK))
    ang = pos[:, None] * inv_freq[None, :]
    return np.cos(ang), np.sin(ang)


def kernel(x_prompt, x_sample, state_ret, state_pool, norm_mix_pre, norm_mix_post, w_in,
           ret_norm_w, w_pool, pool_scale, w_out, norm_ffn_pre, norm_ffn_post,
           w_gate, w_up, w_down):
    bp, lp, d = x_prompt.shape
    bs, ls, _ = x_sample.shape
    mp, ms = bp * lp, bs * ls
    xp2 = x_prompt.reshape(mp, d)
    xs2 = x_sample.reshape(ms, d)

    qkvg, u, (w_out_b, wg_b, wu_b, wd_b) = _inproj(
        xp2, xs2, norm_mix_pre, w_in, (w_out, w_gate, w_up, w_down))

    s0_prompt = jnp.zeros((bp,) + state_ret.shape[1:], F32)
    buf_prompt = jnp.zeros((bp,) + state_pool.shape[1:], F32)
    ret_p, s_p = _retention(qkvg, 0, bp, lp, s0_prompt, 0.0, ret_norm_w, n_seq=1, n_chunks=2)
    pool_p, buf_p = _pool(u, 0, bp, lp, buf_prompt, w_pool, pool_scale, 0.0, bb=1, tl=1024)
    ret_s, s_s = _retention(qkvg, mp, bs, ls, state_ret, float(PAST_LEN), ret_norm_w,
                            n_seq=8, n_chunks=1)
    pool_s, buf_s = _pool(u, mp, bs, ls, state_pool, w_pool, pool_scale, float(PAST_LEN),
                          bb=32, tl=ls)

    x1, h2 = _outproj(ret_p, ret_s, pool_p, pool_s, xp2, xs2, w_out_b,
                      norm_mix_post, norm_ffn_pre)
    act = _ffn_up(h2, wg_b, wu_b, n_col_blocks=2)
    yp, ys = _ffn_down(act, x1, mp, ms, wd_b, norm_ffn_post)

    return (yp.reshape(bp, lp, d), ys.reshape(bs, ls, d),
            s_p.astype(x_prompt.dtype), buf_p.astype(x_prompt.dtype),
            s_s.astype(state_ret.dtype), buf_s.astype(state_pool.dtype))
```
